```python
import math
import jax, jax.numpy as jnp
from jax import lax
import numpy as np

D_MODEL = 1024
BATCH = 8
SEQ = 4096
DEPTH = 1

RET_HEADS = 4
RET_QK_DIM = 128
RET_V_DIM = 256
RET_CHUNK = 128
RET_QK_WIDTH = RET_HEADS * RET_QK_DIM
RET_V_WIDTH = RET_HEADS * RET_V_DIM
ROPE_BASE = 10000.0
SGU_GROUPS = 4
SGU_GROUP_DIM = 256
SGU_CHUNK = 128
SGU_WIDTH = SGU_GROUPS * SGU_GROUP_DIM
IN_WIDTHS = (RET_QK_WIDTH, RET_QK_WIDTH, RET_V_WIDTH, RET_V_WIDTH, SGU_WIDTH, SGU_WIDTH, D_MODEL, D_MODEL)
IN_WIDTH = 7168
IN_SPLITS = (512, 1024, 2048, 3072, 4096, 5120, 6144)
PEER_HEADS = 8
PEER_N_KEYS = 128
PEER_N_EXPERTS = PEER_N_KEYS * PEER_N_KEYS
PEER_KEY_DIM = 256
PEER_HALF = PEER_KEY_DIM // 2
PEER_TOPK = 16
PEER_TOKEN_BLOCK = 128
N_MOD = 6
EPS = 1e-6

kernel_name = "hybrid_retention_sgu_peer_block"


def rms_norm(x, gain):
    xf = x.astype(jnp.float32)
    y = xf * lax.rsqrt(jnp.mean(xf * xf, axis=-1, keepdims=True) + EPS)
    return (y * gain.astype(jnp.float32)).astype(x.dtype)


def modulate(h, shift, scale):
    return h * (1 + scale[:, None, :]) + shift[:, None, :]


def rotary(x, pos):
    d = x.shape[-1]
    half = d // 2
    inv = ROPE_BASE ** (-jnp.arange(half, dtype=jnp.float32) * 2.0 / d)
    ang = pos[:, None] * inv[None, :]
    cos = jnp.cos(ang)[None, :, None, :].astype(x.dtype)
    sin = jnp.sin(ang)[None, :, None, :].astype(x.dtype)
    x1, x2 = x[..., :half], x[..., half:]
    return jnp.concatenate([x1 * cos - x2 * sin, x1 * sin + x2 * cos], axis=-1)


def retention(q, k, v):
    B, S, H, dk = q.shape
    dv = v.shape[-1]
    C = RET_CHUNK
    nc = S // C
    gamma = 1.0 - 2.0 ** (-5.0 - jnp.arange(H, dtype=jnp.float32))
    log_g = jnp.log(gamma)
    idx = jnp.arange(C, dtype=jnp.float32)
    diff = idx[:, None] - idx[None, :]
    decay_in = jnp.where((diff >= 0)[None], jnp.exp(log_g[:, None, None] * jnp.maximum(diff, 0.0)[None]), 0.0).astype(q.dtype)
    decay_q = jnp.exp(log_g[None, :] * (idx[:, None] + 1.0)).astype(q.dtype)[None, :, :, None]
    decay_k = jnp.exp(log_g[None, :] * (C - 1.0 - idx[:, None])).astype(q.dtype)[None, :, :, None]
    decay_chunk = jnp.exp(log_g * C).astype(q.dtype)[None, :, None, None]

    def to_chunks(t):
        return t.reshape(B, nc, C, H, t.shape[-1]).transpose(1, 0, 2, 3, 4)

    def step(state, inp):
        qi, ki, vi = inp
        s = jnp.einsum('bihd,bjhd->bhij', qi, ki) * decay_in
        inner = jnp.einsum('bhij,bjhv->bihv', s, vi)
        cross = jnp.einsum('bihd,bhdv->bihv', qi, state) * decay_q
        new_state = state * decay_chunk + jnp.einsum('bjhd,bjhv->bhdv', ki * decay_k, vi)
        return new_state, inner + cross

    state0 = jnp.zeros((B, H, dk, dv), q.dtype)
    _, y = lax.scan(step, state0, (to_chunks(q), to_chunks(k), to_chunks(v)))
    return y.transpose(1, 0, 2, 3, 4).reshape(B, S, H, dv)


def head_norm(y, gain):
    B, S = y.shape[:2]
    yf = y.astype(jnp.float32)
    mu = jnp.mean(yf, axis=-1, keepdims=True)
    var = jnp.mean(jnp.square(yf - mu), axis=-1, keepdims=True)
    yn = ((yf - mu) * lax.rsqrt(var + EPS)).reshape(B, S, -1)
    return (yn * gain.astype(jnp.float32)).astype(y.dtype)


def spatial_gating(u, sv, ln_g, ln_b, w_s, b_s):
    B, S, _ = u.shape
    vf = sv.astype(jnp.float32)
    mu = jnp.mean(vf, axis=-1, keepdims=True)
    var = jnp.mean(jnp.square(vf - mu), axis=-1, keepdims=True)
    vn = ((vf - mu) * lax.rsqrt(var + EPS) * ln_g.astype(jnp.float32) + ln_b.astype(jnp.float32)).astype(sv.dtype)
    nck = S // SGU_CHUNK
    vn = vn.reshape(B, nck, SGU_CHUNK, SGU_GROUPS, SGU_GROUP_DIM)
    mask = jnp.tril(jnp.ones((SGU_CHUNK, SGU_CHUNK), dtype=bool))
    ws = jnp.where(mask[None], w_s, jnp.zeros_like(w_s))
    mixed = jnp.einsum('gts,bnsgc->bntgc', ws, vn) + b_s.T[None, None, :, :, None]
    return u * mixed.reshape(B, S, SGU_WIDTH)


def peer(h, w_q, sub_keys, expert_u, expert_v):
    B, S, D = h.shape
    blocks = h.reshape(-1, PEER_TOKEN_BLOCK, D)
    K = PEER_TOPK

    def block(hb):
        T = hb.shape[0]
        q = (hb @ w_q).reshape(T, PEER_HEADS, 2, PEER_HALF)
        s = jnp.einsum('thpc,hpnc->thpn', q, sub_keys)
        sub_s, sub_i = lax.top_k(s, K)
        cand = (sub_s[:, :, 0, :, None] + sub_s[:, :, 1, None, :]).reshape(T, PEER_HEADS, K * K)
        top_s, top_i = lax.top_k(cand, K)
        k1 = jnp.take_along_axis(sub_i[:, :, 0], top_i // K, axis=-1)
        k2 = jnp.take_along_axis(sub_i[:, :, 1], top_i % K, axis=-1)
        expert = k1 * PEER_N_KEYS + k2
        g = jax.nn.softmax(top_s.astype(jnp.float32), axis=-1).astype(hb.dtype)
        ue = expert_u[expert]
        act = jax.nn.gelu(jnp.einsum('td,thkd->thk', hb, ue), approximate=False) * g
        ve = expert_v[expert]
        return jnp.einsum('thk,thkd->td', act, ve)

    return lax.map(block, blocks).reshape(B, S, D)


def setup_inputs(seed: int = 0) -> dict:
    key = jax.random.key(seed)
    ks = jax.random.split(key, 21)
    L, D = DEPTH, D_MODEL
    f32 = jnp.float32
    nrm = lambda k, shape, s: jax.random.normal(k, shape, f32) * s
    return {
        "x": nrm(ks[0], (BATCH, SEQ, D), 1.0),
        "c": nrm(ks[1], (BATCH, D), 1.0),
        "w_ada": nrm(ks[2], (L, D, N_MOD * D), 0.5 * D ** -0.5),
        "b_ada": nrm(ks[3], (L, N_MOD * D), 0.01),
        "norm1_g": 1.0 + nrm(ks[4], (L, D), 0.02),
        "w_in": nrm(ks[5], (L, D, IN_WIDTH), D ** -0.5),
        "ret_gn_g": 1.0 + nrm(ks[6], (L, RET_V_WIDTH), 0.02),
        "sgu_ln_g": 1.0 + nrm(ks[7], (L, SGU_WIDTH), 0.02),
        "sgu_ln_b": nrm(ks[8], (L, SGU_WIDTH), 0.01),
        "sgu_w": nrm(ks[9], (L, SGU_GROUPS, SGU_CHUNK, SGU_CHUNK), 0.5 * SGU_CHUNK ** -0.5),
        "sgu_b": 1.0 + nrm(ks[10], (L, SGU_GROUPS, SGU_CHUNK), 0.01),
        "w_ret_out": nrm(ks[11], (L, RET_V_WIDTH, D), RET_V_WIDTH ** -0.5),
        "w_sgu_out": nrm(ks[12], (L, SGU_WIDTH, D), SGU_WIDTH ** -0.5),
        "w_out": nrm(ks[13], (L, D, D), D ** -0.5),
        "norm2_g": 1.0 + nrm(ks[14], (L, D), 0.02),
        "peer_w_q": nrm(ks[15], (L, D, PEER_HEADS * PEER_KEY_DIM), D ** -0.5),
        "peer_sub_keys": nrm(ks[16], (L, PEER_HEADS, 2, PEER_N_KEYS, PEER_HALF), PEER_HALF ** -0.5),
        "peer_u": nrm(ks[17], (L, PEER_N_EXPERTS, D), D ** -0.5),
        "peer_v": nrm(ks[18], (L, PEER_N_EXPERTS, D), 1.0),
        "final_g": 1.0 + nrm(ks[19], (D,), 0.02),
    }


def reference(x, c, w_ada, b_ada, norm1_g, w_in, ret_gn_g, sgu_ln_g, sgu_ln_b, sgu_w, sgu_b, w_ret_out, w_sgu_out, w_out, norm2_g, peer_w_q, peer_sub_keys, peer_u, peer_v, final_g):
    B, S, D = x.shape
    pos = jnp.arange(S, dtype=jnp.float32)
    c_act = jax.nn.silu(c)
    for l in range(DEPTH):
        mod = c_act @ w_ada[l] + b_ada[l]
        shift1, scale1, gate1, shift2, scale2, gate2 = jnp.split(mod, N_MOD, axis=-1)

        h = modulate(rms_norm(x, norm1_g[l]), shift1, scale1)
        proj = h @ w_in[l]
        q, k, v, g_ret, u, sv, gate_a, gate_b = jnp.split(proj, IN_SPLITS, axis=-1)

        q = rotary(q.reshape(B, S, RET_HEADS, RET_QK_DIM), pos)
        k = rotary(k.reshape(B, S, RET_HEADS, RET_QK_DIM), pos) * (RET_QK_DIM ** -0.5)
        v = v.reshape(B, S, RET_HEADS, RET_V_DIM)
        y_ret = head_norm(retention(q, k, v), ret_gn_g[l])
        branch_a = (jax.nn.silu(g_ret) * y_ret) @ w_ret_out[l]

        y_sgu = spatial_gating(jax.nn.gelu(u, approximate=False), jax.nn.gelu(sv, approximate=False),
                               sgu_ln_g[l], sgu_ln_b[l], sgu_w[l], sgu_b[l])
        branch_b = y_sgu @ w_sgu_out[l]

        merged = jax.nn.sigmoid(gate_a) * branch_a + jax.nn.sigmoid(gate_b) * branch_b
        x = x + gate1[:, None, :] * (merged @ w_out[l])

        h2 = modulate(rms_norm(x, norm2_g[l]), shift2, scale2)
        x = x + gate2[:, None, :] * peer(h2, peer_w_q[l], peer_sub_keys[l], peer_u[l], peer_v[l])
    return rms_norm(x, final_g)
```

```python
import functools
import math

import jax
import jax.numpy as jnp
from jax import lax
from jax.experimental import pallas as pl
from jax.experimental.pallas import tpu as pltpu

F32 = jnp.float32
BF16 = jnp.bfloat16

D_MODEL = 1024
RET_HEADS = 4
RET_QK_DIM = 128
RET_V_DIM = 256
CHUNK = 128
SGU_GROUPS = 4
SGU_GROUP_DIM = 256
IN_WIDTH = 7168
PEER_HEADS = 8
PEER_N_KEYS = 128
PEER_TOPK = 16
N_MOD = 6
EPS = 1e-6
ROPE_BASE = 10000.0
NEG_INF = float("-inf")

VMEM_LIMIT_BYTES = 56 * 1024 * 1024

TM_PROJ = 512
TOPK_TOKENS = 256
PEER_TOKENS = 512
PEER_EXPERTS = 1024
LANES = 128
SUBLANES = 8


def _gelu(x):
    return 0.5 * x * (1.0 + lax.erf(x * (1.0 / math.sqrt(2.0))))


def _rms(x, g):
    ms = jnp.mean(x * x, axis=-1, keepdims=True)
    return x * lax.rsqrt(ms + EPS) * g


def _cparams(sem):
    return pltpu.CompilerParams(dimension_semantics=sem, vmem_limit_bytes=VMEM_LIMIT_BYTES)


def _ada_kernel(c_ref, w_ref, b_ref, o_ref):
    c = c_ref[...]
    ca = c * jax.nn.sigmoid(c)
    o_ref[...] = jnp.dot(ca.astype(BF16), w_ref[...].astype(BF16),
                         preferred_element_type=F32) + b_ref[...]


def _ada(c, w, b):
    bsz, d = c.shape
    n_out = w.shape[1]
    return pl.pallas_call(
        _ada_kernel,
        grid=(n_out // d,),
        in_specs=[pl.BlockSpec((bsz, d), lambda j: (0, 0)),
                  pl.BlockSpec((d, d), lambda j: (0, j)),
                  pl.BlockSpec((1, d), lambda j: (0, j))],
        out_specs=pl.BlockSpec((bsz, d), lambda j: (0, j)),
        out_shape=jax.ShapeDtypeStruct((bsz, n_out), F32),
        compiler_params=_cparams(("arbitrary",)),
        name="ada",
    )(c, w, b.reshape(1, n_out))


def _inproj_kernel(x_ref, sh_ref, sc_ref, g_ref, cos_ref, sin_ref, lng_ref, lnb_ref, w_ref, o_ref):
    h = (_rms(x_ref[...], g_ref[...]) * (1.0 + sc_ref[0]) + sh_ref[0]).astype(BF16)
    d = D_MODEL
    for j in range(IN_WIDTH // d):
        acc = jnp.dot(h, w_ref[:, j * d:(j + 1) * d], preferred_element_type=F32)
        if j == 0:
            cos = cos_ref[...]
            sin = sin_ref[...]
            for hh in range(2 * RET_HEADS):
                blk = acc[:, hh * RET_QK_DIM:(hh + 1) * RET_QK_DIM]
                r = blk * cos + pltpu.roll(blk, RET_QK_DIM // 2, 1) * sin
                if hh >= RET_HEADS:
                    r = r * (RET_QK_DIM ** -0.5)
                o_ref[:, hh * RET_QK_DIM:(hh + 1) * RET_QK_DIM] = r.astype(BF16)
            continue
        if j == 1:
            res = acc
        elif j == 2:
            res = acc * jax.nn.sigmoid(acc)
        elif j == 3:
            res = _gelu(acc)
        elif j == 4:
            a = _gelu(acc)
            mu = jnp.mean(a, axis=-1, keepdims=True)
            ac = a - mu
            var = jnp.mean(ac * ac, axis=-1, keepdims=True)
            res = ac * lax.rsqrt(var + EPS) * lng_ref[...] + lnb_ref[...]
        else:
            res = jax.nn.sigmoid(acc)
        o_ref[:, j * d:(j + 1) * d] = res.astype(BF16)


def _inproj(x2d, mod3, g, cos_t, sin_t, ln_g, ln_b, w_bf, seq):
    n, d = x2d.shape
    tm = TM_PROJ
    tpb = seq // tm
    return pl.pallas_call(
        _inproj_kernel,
        grid=(n // tm,),
        in_specs=[pl.BlockSpec((tm, d), lambda i: (i, 0)),
                  pl.BlockSpec((1, 1, d), lambda i: (i // tpb, 0, 0)),
                  pl.BlockSpec((1, 1, d), lambda i: (i // tpb, 0, 1)),
                  pl.BlockSpec((1, d), lambda i: (0, 0)),
                  pl.BlockSpec((tm, RET_QK_DIM), lambda i: (i % tpb, 0)),
                  pl.BlockSpec((tm, RET_QK_DIM), lambda i: (i % tpb, 0)),
                  pl.BlockSpec((1, d), lambda i: (0, 0)),
                  pl.BlockSpec((1, d), lambda i: (0, 0)),
                  pl.BlockSpec((d, IN_WIDTH), lambda i: (0, 0), pipeline_mode=pl.Buffered(1))],
        out_specs=pl.BlockSpec((tm, IN_WIDTH), lambda i: (i, 0)),
        out_shape=jax.ShapeDtypeStruct((n, IN_WIDTH), BF16),
        compiler_params=_cparams(("arbitrary",)),
        name="inproj",
    )(x2d, mod3, mod3, g, cos_t, sin_t, ln_g, ln_b, w_bf)


def _chunk_kernel(qk_ref, v_ref, gs_ref, u_ref, vn_ref, din_ref, dq_ref, dk_ref, dc_ref,
                  gn_ref, ws_ref, bs_ref, ya_ref, ys_ref, state_ref):
    @pl.when(pl.program_id(1) == 0)
    def _():
        state_ref[...] = jnp.zeros_like(state_ref)

    for hh in range(RET_HEADS):
        q = qk_ref[:, hh * RET_QK_DIM:(hh + 1) * RET_QK_DIM]
        k = qk_ref[:, (RET_HEADS + hh) * RET_QK_DIM:(RET_HEADS + hh + 1) * RET_QK_DIM]
        v = v_ref[:, hh * RET_V_DIM:(hh + 1) * RET_V_DIM]
        state = state_ref[hh]
        s = lax.dot_general(q, k, (((1,), (1,)), ((), ())), preferred_element_type=F32) * din_ref[hh]
        inner = jnp.dot(s.astype(BF16), v, preferred_element_type=F32)
        cross = jnp.dot(q, state.astype(BF16), preferred_element_type=F32) * dq_ref[hh]
        kd_t = (k.astype(F32) * dk_ref[hh]).T.astype(BF16)
        state_ref[hh] = state * dc_ref[hh] + jnp.dot(kd_t, v, preferred_element_type=F32)
        y = inner + cross
        mu = jnp.mean(y, axis=-1, keepdims=True)
        yc = y - mu
        var = jnp.mean(yc * yc, axis=-1, keepdims=True)
        sl = slice(hh * RET_V_DIM, (hh + 1) * RET_V_DIM)
        yn = yc * lax.rsqrt(var + EPS) * gn_ref[:, sl]
        ya_ref[:, sl] = (gs_ref[:, sl].astype(F32) * yn).astype(BF16)

    row = lax.broadcasted_iota(jnp.int32, (CHUNK, CHUNK), 0)
    col = lax.broadcasted_iota(jnp.int32, (CHUNK, CHUNK), 1)
    for gg in range(SGU_GROUPS):
        sl = slice(gg * SGU_GROUP_DIM, (gg + 1) * SGU_GROUP_DIM)
        ws = jnp.where(row >= col, ws_ref[gg], 0.0).astype(BF16)
        mixed = jnp.dot(ws, vn_ref[:, sl], preferred_element_type=F32) + bs_ref[gg]
        ys_ref[:, sl] = (u_ref[:, sl].astype(F32) * mixed).astype(BF16)


def _chunk(proj, din, dq, dk, dc, gn, ws, bs, bsz, seq):
    n = proj.shape[0]
    nc = seq // CHUNK
    d = D_MODEL

    def col(j):
        return pl.BlockSpec((CHUNK, d), lambda b, c: (b * nc + c, j))

    def const(shape):
        return pl.BlockSpec(shape, lambda b, c: (0,) * len(shape))

    return pl.pallas_call(
        _chunk_kernel,
        grid=(bsz, nc),
        in_specs=[col(0), col(1), col(2), col(3), col(4),
                  const(din.shape), const(dq.shape), const(dk.shape), const(dc.shape),
                  const(gn.shape), const(ws.shape), const(bs.shape)],
        out_specs=[pl.BlockSpec((CHUNK, d), lambda b, c: (b * nc + c, 0)),
                   pl.BlockSpec((CHUNK, d), lambda b, c: (b * nc + c, 0))],
        out_shape=[jax.ShapeDtypeStruct((n, d), BF16), jax.ShapeDtypeStruct((n, d), BF16)],
        scratch_shapes=[pltpu.VMEM((RET_HEADS, RET_QK_DIM, RET_V_DIM), F32)],
        compiler_params=_cparams(("arbitrary", "arbitrary")),
        name="chunk",
    )(proj, proj, proj, proj, proj, din, dq, dk, dc, gn, ws, bs)


def _merge_kernel(ya_ref, ys_ref, sa_ref, sb_ref, x_ref, g1_ref, sh2_ref, sc2_ref, n2_ref,
                  wa_ref, wb_ref, wo_ref, x1_ref, h2_ref):
    ba = jnp.dot(ya_ref[...], wa_ref[...], preferred_element_type=F32)
    bb = jnp.dot(ys_ref[...], wb_ref[...], preferred_element_type=F32)
    merged = sa_ref[...].astype(F32) * ba + sb_ref[...].astype(F32) * bb
    mo = jnp.dot(merged.astype(BF16), wo_ref[...], preferred_element_type=F32)
    x1 = x_ref[...] + g1_ref[0] * mo
    x1_ref[...] = x1
    h2_ref[...] = (_rms(x1, n2_ref[...]) * (1.0 + sc2_ref[0]) + sh2_ref[0]).astype(BF16)


def _merge(ya, ys, proj, x2d, mod3, n2, wa, wb, wo, seq):
    n, d = x2d.shape
    tm = TM_PROJ
    tpb = seq // tm

    def modspec(k):
        return pl.BlockSpec((1, 1, d), lambda i: (i // tpb, 0, k))

    def wspec():
        return pl.BlockSpec((d, d), lambda i: (0, 0))

    return pl.pallas_call(
        _merge_kernel,
        grid=(n // tm,),
        in_specs=[pl.BlockSpec((tm, d), lambda i: (i, 0)),
                  pl.BlockSpec((tm, d), lambda i: (i, 0)),
                  pl.BlockSpec((tm, d), lambda i: (i, 5)),
                  pl.BlockSpec((tm, d), lambda i: (i, 6)),
                  pl.BlockSpec((tm, d), lambda i: (i, 0)),
                  modspec(2), modspec(3), modspec(4),
                  pl.BlockSpec((1, d), lambda i: (0, 0)),
                  wspec(), wspec(), wspec()],
        out_specs=[pl.BlockSpec((tm, d), lambda i: (i, 0)),
                   pl.BlockSpec((tm, d), lambda i: (i, 0))],
        out_shape=[jax.ShapeDtypeStruct((n, d), F32), jax.ShapeDtypeStruct((n, d), BF16)],
        compiler_params=_cparams(("arbitrary",)),
        name="merge",
    )(ya, ys, proj, proj, x2d, mod3, mod3, mod3, n2, wa, wb, wo)


def _topk_kernel(h2_ref, wq_ref, keys_ref, s1_ref, s2_ref, e1_ref, e2_ref, tau_ref, qt_ref, top_ref):
    nk = PEER_N_KEYS
    ntb = TOPK_TOKENS // LANES
    qt_ref[...] = lax.dot_general(wq_ref[...], h2_ref[...], (((1,), (1,)), ((), ())),
                                  preferred_element_type=F32).astype(BF16)
    s_refs = (s1_ref, s2_ref)

    def head_body(hh, carry):
        sts = []
        for p in range(2):
            off = pl.multiple_of((hh * 2 + p) * nk, nk)
            sts.append(jnp.dot(keys_ref[hh * 2 + p], qt_ref[pl.ds(off, nk), :],
                               preferred_element_type=F32))
        for tb in range(ntb):
            lsl = slice(tb * LANES, (tb + 1) * LANES)
            for p in range(2):
                s = sts[p][:, lsl]
                s_refs[p][tb, hh] = s
                for r in range(PEER_TOPK):
                    m = jnp.max(s, axis=0, keepdims=True)
                    top_ref[p, r:r + 1, lsl] = m
                    s = jnp.where(s == m, NEG_INF, s)
            a1 = top_ref[0, :, lsl]
            a2 = top_ref[1, :, lsl]
            cs = [a1[0:SUBLANES] + a2[0:1], a1[SUBLANES:] + a2[0:1]]
            cs += [a1[0:SUBLANES] + a2[l:l + 1] for l in range(1, PEER_TOPK)]
            m0 = a1[0:1] + a2[0:1]
            z = jnp.zeros_like(m0)
            m = m0
            for r in range(PEER_TOPK):
                m = jnp.max(functools.reduce(jnp.maximum, cs), axis=0, keepdims=True)
                z = z + jnp.exp(m - m0)
                if r + 1 < PEER_TOPK:
                    cs = [jnp.where(c == m, NEG_INF, c) for c in cs]
            tau_ref[tb, hh] = jnp.broadcast_to(m, (SUBLANES, LANES))
            e1_ref[tb, hh] = jnp.exp(sts[0][:, lsl] - a1[0:1]) * (1.0 / z)
            e2_ref[tb, hh] = jnp.exp(sts[1][:, lsl] - a2[0:1])
        return carry

    lax.fori_loop(0, PEER_HEADS, head_body, 0)


def _topk(h2, wq_t, keys):
    n, d = h2.shape
    tt = TOPK_TOKENS
    ntb = tt // LANES
    nb = n // LANES
    big = jax.ShapeDtypeStruct((nb, PEER_HEADS, PEER_N_KEYS, LANES), F32)
    bigspec = pl.BlockSpec((ntb, PEER_HEADS, PEER_N_KEYS, LANES), lambda i: (i, 0, 0, 0))
    return pl.pallas_call(
        _topk_kernel,
        grid=(n // tt,),
        in_specs=[pl.BlockSpec((tt, d), lambda i: (i, 0)),
                  pl.BlockSpec(wq_t.shape, lambda i: (0, 0)),
                  pl.BlockSpec(keys.shape, lambda i: (0, 0, 0))],
        out_specs=[bigspec, bigspec, bigspec, bigspec,
                   pl.BlockSpec((ntb, PEER_HEADS, SUBLANES, LANES), lambda i: (i, 0, 0, 0))],
        out_shape=[big, big, big, big, jax.ShapeDtypeStruct((nb, PEER_HEADS, SUBLANES, LANES), F32)],
        scratch_shapes=[pltpu.VMEM((wq_t.shape[0], tt), BF16),
                        pltpu.VMEM((2, PEER_TOPK, tt), F32)],
        compiler_params=_cparams(("arbitrary",)),
        name="topk",
    )(h2, wq_t, keys)


def _peer_kernel(h2_ref, u_ref, vt_ref, s1_ref, s2_ref, e1_ref, e2_ref, tau_ref, x1_ref, g2_ref, fg_ref,
                 o_ref, acc_ref, st_ref, a_ref):
    e = pl.program_id(1)
    ntb = PEER_TOKENS // LANES
    nk = PEER_N_KEYS
    ni = PEER_EXPERTS // nk

    @pl.when(e == 0)
    def _():
        acc_ref[...] = jnp.zeros_like(acc_ref)

    st_ref[...] = lax.dot_general(u_ref[...], h2_ref[...], (((1,), (1,)), ((), ())),
                                  preferred_element_type=F32)

    def i_body(ii, carry):
        key1 = e * ni + ii
        row0 = pl.multiple_of(ii * nk, nk)
        for tb in range(ntb):
            lsl = slice(tb * LANES, (tb + 1) * LANES)
            g = jnp.zeros((nk, LANES), F32)
            for hh in range(PEER_HEADS):
                s1row = s1_ref[tb, hh, pl.ds(key1, 1), :]
                e1row = e1_ref[tb, hh, pl.ds(key1, 1), :]
                tau = tau_ref[tb, hh, 0:1, :]
                sel = (s2_ref[tb, hh] + s1row) >= tau
                g = g + jnp.where(sel, e2_ref[tb, hh], 0.0) * e1row
            a = _gelu(st_ref[pl.ds(row0, nk), lsl]) * g
            a_ref[pl.ds(row0, nk), lsl] = a.astype(BF16)
        return carry

    lax.fori_loop(0, ni, i_body, 0)

    acc_ref[...] += jnp.dot(vt_ref[...], a_ref[...], preferred_element_type=F32)

    @pl.when(e == pl.num_programs(1) - 1)
    def _():
        x2 = x1_ref[...] + g2_ref[0] * acc_ref[...].T
        o_ref[...] = _rms(x2, fg_ref[...])


def _peer(h2, u_bf, vt_bf, s1, s2, e1, e2, tau, x1, mod3, fg, seq):
    n, d = h2.shape
    tt = PEER_TOKENS
    eb = PEER_EXPERTS
    ntb = tt // LANES
    tpb = seq // tt
    n_exp = u_bf.shape[0]
    bigspec = pl.BlockSpec((ntb, PEER_HEADS, PEER_N_KEYS, LANES), lambda t, e: (t, 0, 0, 0))
    return pl.pallas_call(
        _peer_kernel,
        grid=(n // tt, n_exp // eb),
        in_specs=[pl.BlockSpec((tt, d), lambda t, e: (t, 0)),
                  pl.BlockSpec((eb, d), lambda t, e: (e, 0)),
                  pl.BlockSpec((d, eb), lambda t, e: (0, e)),
                  bigspec, bigspec, bigspec, bigspec,
                  pl.BlockSpec((ntb, PEER_HEADS, SUBLANES, LANES), lambda t, e: (t, 0, 0, 0)),
                  pl.BlockSpec((tt, d), lambda t, e: (t, 0)),
                  pl.BlockSpec((1, 1, d), lambda t, e: (t // tpb, 0, 5)),
                  pl.BlockSpec((1, d), lambda t, e: (0, 0))],
        out_specs=pl.BlockSpec((tt, d), lambda t, e: (t, 0)),
        out_shape=jax.ShapeDtypeStruct((n, d), F32),
        scratch_shapes=[pltpu.VMEM((d, tt), F32),
                        pltpu.VMEM((eb, tt), F32),
                        pltpu.VMEM((eb, tt), BF16)],
        compiler_params=_cparams(("arbitrary", "arbitrary")),
        name="peer",
    )(h2, u_bf, vt_bf, s1, s2, e1, e2, tau, x1, mod3, fg)


def _retention_tables():
    hcount = RET_HEADS
    c = CHUNK
    gamma = 1.0 - 2.0 ** (-5.0 - jnp.arange(hcount, dtype=F32))
    log_g = jnp.log(gamma)
    idx = jnp.arange(c, dtype=F32)
    diff = idx[:, None] - idx[None, :]
    din = jnp.where((diff >= 0)[None], jnp.exp(log_g[:, None, None] * jnp.maximum(diff, 0.0)[None]), 0.0)
    dq = jnp.exp(log_g[:, None] * (idx[None, :] + 1.0))
    dk = jnp.exp(log_g[:, None] * (c - 1.0 - idx[None, :]))
    dc = jnp.exp(log_g * c)
    dq = jnp.broadcast_to(dq[:, :, None], (hcount, c, RET_V_DIM))
    dk = jnp.broadcast_to(dk[:, :, None], (hcount, c, RET_QK_DIM))
    dc = jnp.broadcast_to(dc[:, None, None], (hcount, RET_QK_DIM, RET_V_DIM))
    return din.astype(F32), dq.astype(F32), dk.astype(F32), dc.astype(F32)


def _rotary_tables(seq):
    half = RET_QK_DIM // 2
    pos = jnp.arange(seq, dtype=F32)
    inv = ROPE_BASE ** (-jnp.arange(half, dtype=F32) * 2.0 / RET_QK_DIM)
    ang = pos[:, None] * inv[None, :]
    cos = jnp.cos(ang)
    sin = jnp.sin(ang)
    return jnp.concatenate([cos, cos], axis=-1), jnp.concatenate([-sin, sin], axis=-1)


def kernel(x, c, w_ada, b_ada, norm1_g, w_in, ret_gn_g, sgu_ln_g, sgu_ln_b, sgu_w, sgu_b, w_ret_out,
           w_sgu_out, w_out, norm2_g, peer_w_q, peer_sub_keys, peer_u, peer_v, final_g):
    bsz, seq, d = x.shape
    n = bsz * seq
    depth = w_ada.shape[0]
    assert d == D_MODEL and seq % TM_PROJ == 0 and seq % PEER_TOKENS == 0 and seq % TOPK_TOKENS == 0
    cos_t, sin_t = _rotary_tables(seq)
    din, dq, dk, dc = _retention_tables()
    xc = x.reshape(n, d)
    for l in range(depth):
        mod3 = _ada(c, w_ada[l], b_ada[l]).reshape(bsz, 1, N_MOD * d)
        proj = _inproj(xc, mod3, norm1_g[l].reshape(1, d), cos_t, sin_t,
                       sgu_ln_g[l].reshape(1, d), sgu_ln_b[l].reshape(1, d), w_in[l].astype(BF16), seq)
        bs = jnp.broadcast_to(sgu_b[l][:, :, None], (SGU_GROUPS, CHUNK, SGU_GROUP_DIM))
        ya, ys = _chunk(proj, din, dq, dk, dc, ret_gn_g[l].reshape(1, d), sgu_w[l], bs, bsz, seq)
        x1, h2 = _merge(ya, ys, proj, xc, mod3, norm2_g[l].reshape(1, d), w_ret_out[l].astype(BF16),
                        w_sgu_out[l].astype(BF16), w_out[l].astype(BF16), seq)
        wq_t = peer_w_q[l].T.astype(BF16)
        keys = peer_sub_keys[l].reshape(PEER_HEADS * 2, PEER_N_KEYS, -1).astype(BF16)
        s1, s2, e1, e2, tau = _topk(h2, wq_t, keys)
        assert depth == 1
        xc = _peer(h2, peer_u[l].astype(BF16), peer_v[l].T.astype(BF16), s1, s2, e1, e2, tau, x1, mod3,
                   final_g.reshape(1, d), seq)
    return xc.reshape(bsz, seq, d)
```

```python
import functools
import math

import jax
import jax.numpy as jnp
from jax import lax
from jax.experimental import pallas as pl
from jax.experimental.pallas import tpu as pltpu

F32 = jnp.float32
BF16 = jnp.bfloat16

D_MODEL = 1024
RET_HEADS = 4
RET_QK_DIM = 128
RET_V_DIM = 256
CHUNK = 128
SGU_GROUPS = 4
SGU_GROUP_DIM = 256
IN_WIDTH = 7168
PEER_HEADS = 8
PEER_N_KEYS = 128
PEER_TOPK = 16
N_MOD = 6
EPS = 1e-6
ROPE_BASE = 10000.0
NEG_INF = float("-inf")

VMEM_LIMIT_BYTES = 56 * 1024 * 1024

TM_PROJ = 512
TOPK_TOKENS = 256
PEER_TOKENS = 512
PEER_EXPERTS = 1024
PEER_SUB = 256
PEER_JC = 32
LANES = 128
SUBLANES = 8


def _gelu(x):
    return 0.5 * x * (1.0 + lax.erf(x * (1.0 / math.sqrt(2.0))))


def _rms(x, g):
    ms = jnp.mean(x * x, axis=-1, keepdims=True)
    return x * lax.rsqrt(ms + EPS) * g


def _cparams(sem):
    return pltpu.CompilerParams(dimension_semantics=sem, vmem_limit_bytes=VMEM_LIMIT_BYTES)


def _ada_kernel(c_ref, w_ref, b_ref, o_ref):
    c = c_ref[...]
    ca = c * jax.nn.sigmoid(c)
    o_ref[...] = jnp.dot(ca.astype(BF16), w_ref[...].astype(BF16),
                         preferred_element_type=F32) + b_ref[...]


def _ada(c, w, b):
    bsz, d = c.shape
    n_out = w.shape[1]
    return pl.pallas_call(
        _ada_kernel,
        grid=(n_out // d,),
        in_specs=[pl.BlockSpec((bsz, d), lambda j: (0, 0)),
                  pl.BlockSpec((d, d), lambda j: (0, j)),
                  pl.BlockSpec((1, d), lambda j: (0, j))],
        out_specs=pl.BlockSpec((bsz, d), lambda j: (0, j)),
        out_shape=jax.ShapeDtypeStruct((bsz, n_out), F32),
        compiler_params=_cparams(("arbitrary",)),
        name="ada",
    )(c, w, b.reshape(1, n_out))


def _inproj_kernel(x_ref, sh_ref, sc_ref, g_ref, cos_ref, sin_ref, lng_ref, lnb_ref, w_ref, o_ref):
    h = (_rms(x_ref[...], g_ref[...]) * (1.0 + sc_ref[0]) + sh_ref[0]).astype(BF16)
    d = D_MODEL
    for j in range(IN_WIDTH // d):
        acc = jnp.dot(h, w_ref[:, j * d:(j + 1) * d], preferred_element_type=F32)
        if j == 0:
            cos = cos_ref[...]
            sin = sin_ref[...]
            for hh in range(2 * RET_HEADS):
                blk = acc[:, hh * RET_QK_DIM:(hh + 1) * RET_QK_DIM]
                r = blk * cos + pltpu.roll(blk, RET_QK_DIM // 2, 1) * sin
                if hh >= RET_HEADS:
                    r = r * (RET_QK_DIM ** -0.5)
                o_ref[:, hh * RET_QK_DIM:(hh + 1) * RET_QK_DIM] = r.astype(BF16)
            continue
        if j == 1:
            res = acc
        elif j == 2:
            res = acc * jax.nn.sigmoid(acc)
        elif j == 3:
            res = _gelu(acc)
        elif j == 4:
            a = _gelu(acc)
            mu = jnp.mean(a, axis=-1, keepdims=True)
            ac = a - mu
            var = jnp.mean(ac * ac, axis=-1, keepdims=True)
            res = ac * lax.rsqrt(var + EPS) * lng_ref[...] + lnb_ref[...]
        else:
            res = jax.nn.sigmoid(acc)
        o_ref[:, j * d:(j + 1) * d] = res.astype(BF16)


def _inproj(x2d, mod3, g, cos_t, sin_t, ln_g, ln_b, w_bf, seq):
    n, d = x2d.shape
    tm = TM_PROJ
    tpb = seq // tm
    return pl.pallas_call(
        _inproj_kernel,
        grid=(n // tm,),
        in_specs=[pl.BlockSpec((tm, d), lambda i: (i, 0)),
                  pl.BlockSpec((1, 1, d), lambda i: (i // tpb, 0, 0)),
                  pl.BlockSpec((1, 1, d), lambda i: (i // tpb, 0, 1)),
                  pl.BlockSpec((1, d), lambda i: (0, 0)),
                  pl.BlockSpec((tm, RET_QK_DIM), lambda i: (i % tpb, 0)),
                  pl.BlockSpec((tm, RET_QK_DIM), lambda i: (i % tpb, 0)),
                  pl.BlockSpec((1, d), lambda i: (0, 0)),
                  pl.BlockSpec((1, d), lambda i: (0, 0)),
                  pl.BlockSpec((d, IN_WIDTH), lambda i: (0, 0), pipeline_mode=pl.Buffered(1))],
        out_specs=pl.BlockSpec((tm, IN_WIDTH), lambda i: (i, 0)),
        out_shape=jax.ShapeDtypeStruct((n, IN_WIDTH), BF16),
        compiler_params=_cparams(("arbitrary",)),
        name="inproj",
    )(x2d, mod3, mod3, g, cos_t, sin_t, ln_g, ln_b, w_bf)


def _chunk_kernel(qk_ref, v_ref, gs_ref, u_ref, vn_ref, din_ref, dq_ref, dk_ref, dc_ref,
                  gn_ref, ws_ref, bs_ref, ya_ref, ys_ref, state_ref):
    @pl.when(pl.program_id(1) == 0)
    def _():
        state_ref[...] = jnp.zeros_like(state_ref)

    for hh in range(RET_HEADS):
        q = qk_ref[:, hh * RET_QK_DIM:(hh + 1) * RET_QK_DIM]
        k = qk_ref[:, (RET_HEADS + hh) * RET_QK_DIM:(RET_HEADS + hh + 1) * RET_QK_DIM]
        v = v_ref[:, hh * RET_V_DIM:(hh + 1) * RET_V_DIM]
        state = state_ref[hh]
        s = lax.dot_general(q, k, (((1,), (1,)), ((), ())), preferred_element_type=F32) * din_ref[hh]
        inner = jnp.dot(s.astype(BF16), v, preferred_element_type=F32)
        cross = jnp.dot(q, state.astype(BF16), preferred_element_type=F32) * dq_ref[hh]
        kd_t = (k.astype(F32) * dk_ref[hh]).T.astype(BF16)
        state_ref[hh] = state * dc_ref[hh] + jnp.dot(kd_t, v, preferred_element_type=F32)
        y = inner + cross
        mu = jnp.mean(y, axis=-1, keepdims=True)
        yc = y - mu
        var = jnp.mean(yc * yc, axis=-1, keepdims=True)
        sl = slice(hh * RET_V_DIM, (hh + 1) * RET_V_DIM)
        yn = yc * lax.rsqrt(var + EPS) * gn_ref[:, sl]
        ya_ref[:, sl] = (gs_ref[:, sl].astype(F32) * yn).astype(BF16)

    row = lax.broadcasted_iota(jnp.int32, (CHUNK, CHUNK), 0)
    col = lax.broadcasted_iota(jnp.int32, (CHUNK, CHUNK), 1)
    for gg in range(SGU_GROUPS):
        sl = slice(gg * SGU_GROUP_DIM, (gg + 1) * SGU_GROUP_DIM)
        ws = jnp.where(row >= col, ws_ref[gg], 0.0).astype(BF16)
        mixed = jnp.dot(ws, vn_ref[:, sl], preferred_element_type=F32) + bs_ref[gg]
        ys_ref[:, sl] = (u_ref[:, sl].astype(F32) * mixed).astype(BF16)


def _chunk(proj, din, dq, dk, dc, gn, ws, bs, bsz, seq):
    n = proj.shape[0]
    nc = seq // CHUNK
    d = D_MODEL

    def col(j):
        return pl.BlockSpec((CHUNK, d), lambda b, c: (b * nc + c, j))

    def const(shape):
        return pl.BlockSpec(shape, lambda b, c: (0,) * len(shape))

    return pl.pallas_call(
        _chunk_kernel,
        grid=(bsz, nc),
        in_specs=[col(0), col(1), col(2), col(3), col(4),
                  const(din.shape), const(dq.shape), const(dk.shape), const(dc.shape),
                  const(gn.shape), const(ws.shape), const(bs.shape)],
        out_specs=[pl.BlockSpec((CHUNK, d), lambda b, c: (b * nc + c, 0)),
                   pl.BlockSpec((CHUNK, d), lambda b, c: (b * nc + c, 0))],
        out_shape=[jax.ShapeDtypeStruct((n, d), BF16), jax.ShapeDtypeStruct((n, d), BF16)],
        scratch_shapes=[pltpu.VMEM((RET_HEADS, RET_QK_DIM, RET_V_DIM), F32)],
        compiler_params=_cparams(("arbitrary", "arbitrary")),
        name="chunk",
    )(proj, proj, proj, proj, proj, din, dq, dk, dc, gn, ws, bs)


def _merge_kernel(ya_ref, ys_ref, sa_ref, sb_ref, x_ref, g1_ref, sh2_ref, sc2_ref, n2_ref,
                  wa_ref, wb_ref, wo_ref, x1_ref, h2t_ref):
    ba = jnp.dot(ya_ref[...], wa_ref[...], preferred_element_type=F32)
    bb = jnp.dot(ys_ref[...], wb_ref[...], preferred_element_type=F32)
    merged = sa_ref[...].astype(F32) * ba + sb_ref[...].astype(F32) * bb
    mo = jnp.dot(merged.astype(BF16), wo_ref[...], preferred_element_type=F32)
    x1 = x_ref[...] + g1_ref[0] * mo
    x1_ref[...] = x1
    h2t_ref[...] = (_rms(x1, n2_ref[...]) * (1.0 + sc2_ref[0]) + sh2_ref[0]).T.astype(BF16)


def _merge(ya, ys, proj, x2d, mod3, n2, wa, wb, wo, seq):
    n, d = x2d.shape
    tm = TM_PROJ
    tpb = seq // tm

    def modspec(k):
        return pl.BlockSpec((1, 1, d), lambda i: (i // tpb, 0, k))

    def wspec():
        return pl.BlockSpec((d, d), lambda i: (0, 0))

    return pl.pallas_call(
        _merge_kernel,
        grid=(n // tm,),
        in_specs=[pl.BlockSpec((tm, d), lambda i: (i, 0)),
                  pl.BlockSpec((tm, d), lambda i: (i, 0)),
                  pl.BlockSpec((tm, d), lambda i: (i, 5)),
                  pl.BlockSpec((tm, d), lambda i: (i, 6)),
                  pl.BlockSpec((tm, d), lambda i: (i, 0)),
                  modspec(2), modspec(3), modspec(4),
                  pl.BlockSpec((1, d), lambda i: (0, 0)),
                  wspec(), wspec(), wspec()],
        out_specs=[pl.BlockSpec((tm, d), lambda i: (i, 0)),
                   pl.BlockSpec((d, tm), lambda i: (0, i))],
        out_shape=[jax.ShapeDtypeStruct((n, d), F32), jax.ShapeDtypeStruct((d, n), BF16)],
        compiler_params=_cparams(("arbitrary",)),
        name="merge",
    )(ya, ys, proj, proj, x2d, mod3, mod3, mod3, n2, wa, wb, wo)


def _sort_network(n):
    pairs = []
    p = 1
    while p < n:
        k = p
        while k >= 1:
            for j in range(k % p, n - k, 2 * k):
                for i in range(min(k, n - j - k)):
                    if (i + j) // (2 * p) == (i + j + k) // (2 * p):
                        pairs.append((i + j, i + j + k))
            k //= 2
        p *= 2
    return pairs


def _sort_desc(vs):
    vs = list(vs)
    for i, j in _sort_network(PEER_TOPK):
        if j < len(vs):
            vs[i], vs[j] = jnp.maximum(vs[i], vs[j]), jnp.minimum(vs[i], vs[j])
    return vs


def _merge_top16_over_sublanes(vs):
    n = PEER_TOPK
    vs = list(vs) + [None] * (n - len(vs))
    for shift in (4, 2, 1):
        w = []
        for k in range(n):
            a, b = vs[k], vs[n - 1 - k]
            b = None if b is None else pltpu.roll(b, shift, 0)
            w.append(a if b is None else (b if a is None else jnp.maximum(a, b)))
        d = n // 2
        while d >= 1:
            for k in range(n):
                if k & d == 0:
                    w[k], w[k + d] = jnp.maximum(w[k], w[k + d]), jnp.minimum(w[k], w[k + d])
            d //= 2
        vs = w
    return vs


def _topk_kernel(h2t_ref, wq_ref, keys_ref, thr_ref, s2_ref, e1_ref, e2_ref, qt_ref):
    nk = PEER_N_KEYS
    ntb = TOPK_TOKENS // LANES
    nv = nk // SUBLANES
    qt_ref[...] = jnp.dot(wq_ref[...], h2t_ref[...], preferred_element_type=F32).astype(BF16)

    def pack(rep, sub_iota):
        out = rep[SUBLANES - 1]
        for r in range(SUBLANES - 2, -1, -1):
            out = jnp.where(sub_iota == r, rep[r], out)
        return out

    def head_body(hh, carry):
        sub_iota = lax.broadcasted_iota(jnp.int32, (SUBLANES, LANES), 0)
        sts = []
        for p in range(2):
            off = pl.multiple_of((hh * 2 + p) * nk, nk)
            sts.append(jnp.dot(keys_ref[hh * 2 + p], qt_ref[pl.ds(off, nk), :],
                               preferred_element_type=F32))
        for tb in range(ntb):
            lsl = slice(tb * LANES, (tb + 1) * LANES)
            rows = [[sts[p][r * SUBLANES:(r + 1) * SUBLANES, lsl] for r in range(nv)] for p in range(2)]
            a1 = _merge_top16_over_sublanes(_sort_desc(rows[0]))
            a2 = _merge_top16_over_sublanes(_sort_desc(rows[1]))
            a1lo, a1hi = pack(a1[:SUBLANES], sub_iota), pack(a1[SUBLANES:], sub_iota)
            a2lo, a2hi = pack(a2[:SUBLANES], sub_iota), pack(a2[SUBLANES:], sub_iota)
            tail = sub_iota >= 2
            cands = [a1lo + a2[0], a1hi + a2[0], a1lo + a2[1],
                     jnp.where(tail, a1[0] + a2lo, NEG_INF), a1[0] + a2hi,
                     jnp.where(tail, a1[1] + a2lo, NEG_INF)]
            cands += [jnp.where(tail, a1lo + a2[l], NEG_INF) for l in (2, 3, 4)]
            top = _merge_top16_over_sublanes(_sort_desc(cands))
            tau = top[PEER_TOPK - 1]
            z = functools.reduce(lambda x, y: x + y, [jnp.exp(t - top[0]) for t in top])
            rz = 1.0 / z
            for r in range(nv):
                rsl = slice(r * SUBLANES, (r + 1) * SUBLANES)
                s1v, s2v = rows[0][r], rows[1][r]
                thr = jnp.full((SUBLANES, LANES), jnp.inf, F32)
                for l in range(PEER_TOPK):
                    thr = jnp.where(s1v + a2[l] >= tau, a2[l], thr)
                thr_ref[tb, hh, rsl, :] = thr
                s2_ref[tb, hh, rsl, :] = s2v
                e1_ref[tb, hh, rsl, :] = jnp.exp(s1v - a1[0]) * rz
                e2_ref[tb, hh, rsl, :] = jnp.exp(s2v - a2[0])
        return carry

    lax.fori_loop(0, PEER_HEADS, head_body, 0)


def _topk(h2t, wq_t, keys):
    d, n = h2t.shape
    tt = TOPK_TOKENS
    ntb = tt // LANES
    nb = n // LANES
    big = jax.ShapeDtypeStruct((nb, PEER_HEADS, PEER_N_KEYS, LANES), F32)
    bigspec = pl.BlockSpec((ntb, PEER_HEADS, PEER_N_KEYS, LANES), lambda i: (i, 0, 0, 0))
    return pl.pallas_call(
        _topk_kernel,
        grid=(n // tt,),
        in_specs=[pl.BlockSpec((d, tt), lambda i: (0, i)),
                  pl.BlockSpec(wq_t.shape, lambda i: (0, 0)),
                  pl.BlockSpec(keys.shape, lambda i: (0, 0, 0))],
        out_specs=[bigspec, bigspec, bigspec, bigspec],
        out_shape=[big, big, big, big],
        scratch_shapes=[pltpu.VMEM((wq_t.shape[0], tt), BF16)],
        compiler_params=_cparams(("arbitrary",)),
        name="topk",
    )(h2t, wq_t, keys)


def _peer_kernel(h2t_ref, u_ref, vt_ref, thr_ref, s2_ref, e1_ref, e2_ref, x1_ref, g2_ref, fg_ref,
                 o_ref, acc_ref, st0_ref, st1_ref, a0_ref, a1_ref):
    e = pl.program_id(1)
    ntb = PEER_TOKENS // LANES
    nk = PEER_N_KEYS
    sub = PEER_SUB
    nsub = PEER_EXPERTS // sub
    st_refs = (st0_ref, st1_ref)
    a_refs = (a0_ref, a1_ref)

    @pl.when(e == 0)
    def _():
        acc_ref[...] = jnp.zeros_like(acc_ref)

    def scores(s):
        st_refs[s % 2][...] = jnp.dot(u_ref[s * sub:(s + 1) * sub, :], h2t_ref[...],
                                      preferred_element_type=F32)

    def gates(s):
        ni = sub // nk
        key0 = e * (PEER_EXPERTS // nk) + s * ni
        njc = nk // PEER_JC
        for tb in range(ntb):
            lsl = slice(tb * LANES, (tb + 1) * LANES)
            gs = [[jnp.zeros((PEER_JC, LANES), F32) for _ in range(njc)] for _ in range(ni)]
            for hh in range(PEER_HEADS):
                rows = [(thr_ref[tb, hh, pl.ds(key0 + ii, 1), :], e1_ref[tb, hh, pl.ds(key0 + ii, 1), :])
                        for ii in range(ni)]
                for jc in range(njc):
                    jsl = slice(jc * PEER_JC, (jc + 1) * PEER_JC)
                    s2c = s2_ref[tb, hh, jsl, :]
                    e2c = e2_ref[tb, hh, jsl, :]
                    for ii in range(ni):
                        gs[ii][jc] = gs[ii][jc] + jnp.where(s2c >= rows[ii][0], e2c, 0.0) * rows[ii][1]
            for ii in range(ni):
                for jc in range(njc):
                    rsl = slice(ii * nk + jc * PEER_JC, ii * nk + (jc + 1) * PEER_JC)
                    a = _gelu(st_refs[s % 2][rsl, lsl]) * gs[ii][jc]
                    a_refs[s % 2][rsl, lsl] = a.astype(BF16)

    def combine(s):
        acc_ref[...] += jnp.dot(vt_ref[:, s * sub:(s + 1) * sub], a_refs[s % 2][...],
                                preferred_element_type=F32)

    scores(0)
    for s in range(nsub):
        if s + 1 < nsub:
            scores(s + 1)
        if s > 0:
            combine(s - 1)
        gates(s)
    combine(nsub - 1)

    @pl.when(e == pl.num_programs(1) - 1)
    def _():
        x2 = x1_ref[...] + g2_ref[0] * acc_ref[...].T
        o_ref[...] = _rms(x2, fg_ref[...])


def _peer(h2t, u_bf, vt_bf, thr, s2, e1, e2, x1, mod3, fg, seq):
    d, n = h2t.shape
    tt = PEER_TOKENS
    eb = PEER_EXPERTS
    ntb = tt // LANES
    tpb = seq // tt
    n_exp = u_bf.shape[0]
    bigspec = pl.BlockSpec((ntb, PEER_HEADS, PEER_N_KEYS, LANES), lambda t, e: (t, 0, 0, 0))
    return pl.pallas_call(
        _peer_kernel,
        grid=(n // tt, n_exp // eb),
        in_specs=[pl.BlockSpec((d, tt), lambda t, e: (0, t)),
                  pl.BlockSpec((eb, d), lambda t, e: (e, 0)),
                  pl.BlockSpec((d, eb), lambda t, e: (0, e)),
                  bigspec, bigspec, bigspec, bigspec,
                  pl.BlockSpec((tt, d), lambda t, e: (t, 0)),
                  pl.BlockSpec((1, 1, d), lambda t, e: (t // tpb, 0, 5)),
                  pl.BlockSpec((1, d), lambda t, e: (0, 0))],
        out_specs=pl.BlockSpec((tt, d), lambda t, e: (t, 0)),
        out_shape=jax.ShapeDtypeStruct((n, d), F32),
        scratch_shapes=[pltpu.VMEM((d, tt), F32),
                        pltpu.VMEM((PEER_SUB, tt), F32), pltpu.VMEM((PEER_SUB, tt), F32),
                        pltpu.VMEM((PEER_SUB, tt), BF16), pltpu.VMEM((PEER_SUB, tt), BF16)],
        compiler_params=_cparams(("arbitrary", "arbitrary")),
        name="peer",
    )(h2t, u_bf, vt_bf, thr, s2, e1, e2, x1, mod3, fg)


def _retention_tables():
    hcount = RET_HEADS
    c = CHUNK
    gamma = 1.0 - 2.0 ** (-5.0 - jnp.arange(hcount, dtype=F32))
    log_g = jnp.log(gamma)
    idx = jnp.arange(c, dtype=F32)
    diff = idx[:, None] - idx[None, :]
    din = jnp.where((diff >= 0)[None], jnp.exp(log_g[:, None, None] * jnp.maximum(diff, 0.0)[None]), 0.0)
    dq = jnp.exp(log_g[:, None] * (idx[None, :] + 1.0))
    dk = jnp.exp(log_g[:, None] * (c - 1.0 - idx[None, :]))
    dc = jnp.exp(log_g * c)
    dq = jnp.broadcast_to(dq[:, :, None], (hcount, c, RET_V_DIM))
    dk = jnp.broadcast_to(dk[:, :, None], (hcount, c, RET_QK_DIM))
    dc = jnp.broadcast_to(dc[:, None, None], (hcount, RET_QK_DIM, RET_V_DIM))
    return din.astype(F32), dq.astype(F32), dk.astype(F32), dc.astype(F32)


def _rotary_tables(seq):
    half = RET_QK_DIM // 2
    pos = jnp.arange(seq, dtype=F32)
    inv = ROPE_BASE ** (-jnp.arange(half, dtype=F32) * 2.0 / RET_QK_DIM)
    ang = pos[:, None] * inv[None, :]
    cos = jnp.cos(ang)
    sin = jnp.sin(ang)
    return jnp.concatenate([cos, cos], axis=-1), jnp.concatenate([-sin, sin], axis=-1)


def kernel(x, c, w_ada, b_ada, norm1_g, w_in, ret_gn_g, sgu_ln_g, sgu_ln_b, sgu_w, sgu_b, w_ret_out,
           w_sgu_out, w_out, norm2_g, peer_w_q, peer_sub_keys, peer_u, peer_v, final_g):
    bsz, seq, d = x.shape
    n = bsz * seq
    depth = w_ada.shape[0]
    assert d == D_MODEL and seq % TM_PROJ == 0 and seq % PEER_TOKENS == 0 and seq % TOPK_TOKENS == 0
    cos_t, sin_t = _rotary_tables(seq)
    din, dq, dk, dc = _retention_tables()
    xc = x.reshape(n, d)
    for l in range(depth):
        mod3 = _ada(c, w_ada[l], b_ada[l]).reshape(bsz, 1, N_MOD * d)
        proj = _inproj(xc, mod3, norm1_g[l].reshape(1, d), cos_t, sin_t,
                       sgu_ln_g[l].reshape(1, d), sgu_ln_b[l].reshape(1, d), w_in[l].astype(BF16), seq)
        bs = jnp.broadcast_to(sgu_b[l][:, :, None], (SGU_GROUPS, CHUNK, SGU_GROUP_DIM))
        ya, ys = _chunk(proj, din, dq, dk, dc, ret_gn_g[l].reshape(1, d), sgu_w[l], bs, bsz, seq)
        x1, h2t = _merge(ya, ys, proj, xc, mod3, norm2_g[l].reshape(1, d), w_ret_out[l].astype(BF16),
                        w_sgu_out[l].astype(BF16), w_out[l].astype(BF16), seq)
        wq_t = peer_w_q[l].T.astype(BF16)
        keys = peer_sub_keys[l].reshape(PEER_HEADS * 2, PEER_N_KEYS, -1).astype(BF16)
        thr, s2, e1, e2 = _topk(h2t, wq_t, keys)
        assert depth == 1
        xc = _peer(h2t, peer_u[l].astype(BF16), peer_v[l].T.astype(BF16), thr, s2, e1, e2, x1, mod3,
                   final_g.reshape(1, d), seq)
    return xc.reshape(bsz, seq, d)
```

```python
import functools
import math

import jax
import jax.numpy as jnp
from jax import lax
from jax.experimental import pallas as pl
from jax.experimental.pallas import tpu as pltpu

F32 = jnp.float32
BF16 = jnp.bfloat16

D_MODEL = 1024
RET_HEADS = 4
RET_QK_DIM = 128
RET_V_DIM = 256
CHUNK = 128
SGU_GROUPS = 4
SGU_GROUP_DIM = 256
IN_WIDTH = 7168
PEER_HEADS = 8
PEER_N_KEYS = 128
PEER_TOPK = 16
N_MOD = 6
EPS = 1e-6
ROPE_BASE = 10000.0
NEG_INF = float("-inf")

VMEM_LIMIT_BYTES = 56 * 1024 * 1024

TM_PROJ = 512
TOPK_TOKENS = 256
PEER_TOKENS = 512
PEER_EXPERTS = 1024
PEER_GROUP = 512
PEER_PASS = 256
PEER_JC = 64
LANES = 128
SUBLANES = 8


def _gelu(x):
    return 0.5 * x * (1.0 + lax.erf(x * (1.0 / math.sqrt(2.0))))


def _rms(x, g):
    ms = jnp.mean(x * x, axis=-1, keepdims=True)
    return x * lax.rsqrt(ms + EPS) * g


def _cparams(sem):
    return pltpu.CompilerParams(dimension_semantics=sem, vmem_limit_bytes=VMEM_LIMIT_BYTES)


def _ada_kernel(c_ref, w_ref, b_ref, o_ref):
    c = c_ref[...]
    ca = c * jax.nn.sigmoid(c)
    o_ref[...] = jnp.dot(ca.astype(BF16), w_ref[...].astype(BF16),
                         preferred_element_type=F32) + b_ref[...]


def _ada(c, w, b):
    bsz, d = c.shape
    n_out = w.shape[1]
    return pl.pallas_call(
        _ada_kernel,
        grid=(n_out // d,),
        in_specs=[pl.BlockSpec((bsz, d), lambda j: (0, 0)),
                  pl.BlockSpec((d, d), lambda j: (0, j)),
                  pl.BlockSpec((1, d), lambda j: (0, j))],
        out_specs=pl.BlockSpec((bsz, d), lambda j: (0, j)),
        out_shape=jax.ShapeDtypeStruct((bsz, n_out), F32),
        compiler_params=_cparams(("arbitrary",)),
        name="ada",
    )(c, w, b.reshape(1, n_out))


def _inproj_kernel(x_ref, sh_ref, sc_ref, g_ref, cos_ref, sin_ref, lng_ref, lnb_ref, w_ref, o_ref):
    h = (_rms(x_ref[...], g_ref[...]) * (1.0 + sc_ref[0]) + sh_ref[0]).astype(BF16)
    d = D_MODEL
    for j in range(IN_WIDTH // d):
        acc = jnp.dot(h, w_ref[:, j * d:(j + 1) * d], preferred_element_type=F32)
        if j == 0:
            cos = cos_ref[...]
            sin = sin_ref[...]
            for hh in range(2 * RET_HEADS):
                blk = acc[:, hh * RET_QK_DIM:(hh + 1) * RET_QK_DIM]
                r = blk * cos + pltpu.roll(blk, RET_QK_DIM // 2, 1) * sin
                if hh >= RET_HEADS:
                    r = r * (RET_QK_DIM ** -0.5)
                o_ref[:, hh * RET_QK_DIM:(hh + 1) * RET_QK_DIM] = r.astype(BF16)
            continue
        if j == 1:
            res = acc
        elif j == 2:
            res = acc * jax.nn.sigmoid(acc)
        elif j == 3:
            res = _gelu(acc)
        elif j == 4:
            a = _gelu(acc)
            mu = jnp.mean(a, axis=-1, keepdims=True)
            ac = a - mu
            var = jnp.mean(ac * ac, axis=-1, keepdims=True)
            res = ac * lax.rsqrt(var + EPS) * lng_ref[...] + lnb_ref[...]
        else:
            res = jax.nn.sigmoid(acc)
        o_ref[:, j * d:(j + 1) * d] = res.astype(BF16)


def _inproj(x2d, mod3, g, cos_t, sin_t, ln_g, ln_b, w_bf, seq):
    n, d = x2d.shape
    tm = TM_PROJ
    tpb = seq // tm
    return pl.pallas_call(
        _inproj_kernel,
        grid=(n // tm,),
        in_specs=[pl.BlockSpec((tm, d), lambda i: (i, 0)),
                  pl.BlockSpec((1, 1, d), lambda i: (i // tpb, 0, 0)),
                  pl.BlockSpec((1, 1, d), lambda i: (i // tpb, 0, 1)),
                  pl.BlockSpec((1, d), lambda i: (0, 0)),
                  pl.BlockSpec((tm, RET_QK_DIM), lambda i: (i % tpb, 0)),
                  pl.BlockSpec((tm, RET_QK_DIM), lambda i: (i % tpb, 0)),
                  pl.BlockSpec((1, d), lambda i: (0, 0)),
                  pl.BlockSpec((1, d), lambda i: (0, 0)),
                  pl.BlockSpec((d, IN_WIDTH), lambda i: (0, 0), pipeline_mode=pl.Buffered(1))],
        out_specs=pl.BlockSpec((tm, IN_WIDTH), lambda i: (i, 0)),
        out_shape=jax.ShapeDtypeStruct((n, IN_WIDTH), BF16),
        compiler_params=_cparams(("arbitrary",)),
        name="inproj",
    )(x2d, mod3, mod3, g, cos_t, sin_t, ln_g, ln_b, w_bf)


def _chunk_kernel(qk_ref, v_ref, gs_ref, u_ref, vn_ref, din_ref, dq_ref, dk_ref, dc_ref,
                  gn_ref, ws_ref, bs_ref, ya_ref, ys_ref, state_ref):
    @pl.when(pl.program_id(1) == 0)
    def _():
        state_ref[...] = jnp.zeros_like(state_ref)

    for hh in range(RET_HEADS):
        q = qk_ref[:, hh * RET_QK_DIM:(hh + 1) * RET_QK_DIM]
        k = qk_ref[:, (RET_HEADS + hh) * RET_QK_DIM:(RET_HEADS + hh + 1) * RET_QK_DIM]
        v = v_ref[:, hh * RET_V_DIM:(hh + 1) * RET_V_DIM]
        state = state_ref[hh]
        s = lax.dot_general(q, k, (((1,), (1,)), ((), ())), preferred_element_type=F32) * din_ref[hh]
        inner = jnp.dot(s.astype(BF16), v, preferred_element_type=F32)
        cross = jnp.dot(q, state.astype(BF16), preferred_element_type=F32) * dq_ref[hh]
        kd_t = (k.astype(F32) * dk_ref[hh]).T.astype(BF16)
        state_ref[hh] = state * dc_ref[hh] + jnp.dot(kd_t, v, preferred_element_type=F32)
        y = inner + cross
        mu = jnp.mean(y, axis=-1, keepdims=True)
        yc = y - mu
        var = jnp.mean(yc * yc, axis=-1, keepdims=True)
        sl = slice(hh * RET_V_DIM, (hh + 1) * RET_V_DIM)
        yn = yc * lax.rsqrt(var + EPS) * gn_ref[:, sl]
        ya_ref[:, sl] = (gs_ref[:, sl].astype(F32) * yn).astype(BF16)

    row = lax.broadcasted_iota(jnp.int32, (CHUNK, CHUNK), 0)
    col = lax.broadcasted_iota(jnp.int32, (CHUNK, CHUNK), 1)
    for gg in range(SGU_GROUPS):
        sl = slice(gg * SGU_GROUP_DIM, (gg + 1) * SGU_GROUP_DIM)
        ws = jnp.where(row >= col, ws_ref[gg], 0.0).astype(BF16)
        mixed = jnp.dot(ws, vn_ref[:, sl], preferred_element_type=F32) + bs_ref[gg]
        ys_ref[:, sl] = (u_ref[:, sl].astype(F32) * mixed).astype(BF16)


def _chunk(proj, din, dq, dk, dc, gn, ws, bs, bsz, seq):
    n = proj.shape[0]
    nc = seq // CHUNK
    d = D_MODEL

    def col(j):
        return pl.BlockSpec((CHUNK, d), lambda b, c: (b * nc + c, j))

    def const(shape):
        return pl.BlockSpec(shape, lambda b, c: (0,) * len(shape))

    return pl.pallas_call(
        _chunk_kernel,
        grid=(bsz, nc),
        in_specs=[col(0), col(1), col(2), col(3), col(4),
                  const(din.shape), const(dq.shape), const(dk.shape), const(dc.shape),
                  const(gn.shape), const(ws.shape), const(bs.shape)],
        out_specs=[pl.BlockSpec((CHUNK, d), lambda b, c: (b * nc + c, 0)),
                   pl.BlockSpec((CHUNK, d), lambda b, c: (b * nc + c, 0))],
        out_shape=[jax.ShapeDtypeStruct((n, d), BF16), jax.ShapeDtypeStruct((n, d), BF16)],
        scratch_shapes=[pltpu.VMEM((RET_HEADS, RET_QK_DIM, RET_V_DIM), F32)],
        compiler_params=_cparams(("arbitrary", "arbitrary")),
        name="chunk",
    )(proj, proj, proj, proj, proj, din, dq, dk, dc, gn, ws, bs)


def _merge_kernel(ya_ref, ys_ref, sa_ref, sb_ref, x_ref, g1_ref, sh2_ref, sc2_ref, n2_ref,
                  wa_ref, wb_ref, wo_ref, x1_ref, h2t_ref):
    ba = jnp.dot(ya_ref[...], wa_ref[...], preferred_element_type=F32)
    bb = jnp.dot(ys_ref[...], wb_ref[...], preferred_element_type=F32)
    merged = sa_ref[...].astype(F32) * ba + sb_ref[...].astype(F32) * bb
    mo = jnp.dot(merged.astype(BF16), wo_ref[...], preferred_element_type=F32)
    x1 = x_ref[...] + g1_ref[0] * mo
    x1_ref[...] = x1
    h2t_ref[...] = (_rms(x1, n2_ref[...]) * (1.0 + sc2_ref[0]) + sh2_ref[0]).T.astype(BF16)


def _merge(ya, ys, proj, x2d, mod3, n2, wa, wb, wo, seq):
    n, d = x2d.shape
    tm = TM_PROJ
    tpb = seq // tm

    def modspec(k):
        return pl.BlockSpec((1, 1, d), lambda i: (i // tpb, 0, k))

    def wspec():
        return pl.BlockSpec((d, d), lambda i: (0, 0))

    return pl.pallas_call(
        _merge_kernel,
        grid=(n // tm,),
        in_specs=[pl.BlockSpec((tm, d), lambda i: (i, 0)),
                  pl.BlockSpec((tm, d), lambda i: (i, 0)),
                  pl.BlockSpec((tm, d), lambda i: (i, 5)),
                  pl.BlockSpec((tm, d), lambda i: (i, 6)),
                  pl.BlockSpec((tm, d), lambda i: (i, 0)),
                  modspec(2), modspec(3), modspec(4),
                  pl.BlockSpec((1, d), lambda i: (0, 0)),
                  wspec(), wspec(), wspec()],
        out_specs=[pl.BlockSpec((tm, d), lambda i: (i, 0)),
                   pl.BlockSpec((d, tm), lambda i: (0, i))],
        out_shape=[jax.ShapeDtypeStruct((n, d), F32), jax.ShapeDtypeStruct((d, n), BF16)],
        compiler_params=_cparams(("arbitrary",)),
        name="merge",
    )(ya, ys, proj, proj, x2d, mod3, mod3, mod3, n2, wa, wb, wo)


def _sort_network(n):
    pairs = []
    p = 1
    while p < n:
        k = p
        while k >= 1:
            for j in range(k % p, n - k, 2 * k):
                for i in range(min(k, n - j - k)):
                    if (i + j) // (2 * p) == (i + j + k) // (2 * p):
                        pairs.append((i + j, i + j + k))
            k //= 2
        p *= 2
    return pairs


def _sort_desc(vs):
    vs = list(vs)
    for i, j in _sort_network(PEER_TOPK):
        if j < len(vs):
            vs[i], vs[j] = jnp.maximum(vs[i], vs[j]), jnp.minimum(vs[i], vs[j])
    return vs


def _merge_top16_over_sublanes(vs):
    n = PEER_TOPK
    vs = list(vs) + [None] * (n - len(vs))
    for shift in (4, 2, 1):
        w = []
        for k in range(n):
            a, b = vs[k], vs[n - 1 - k]
            b = None if b is None else pltpu.roll(b, shift, 0)
            w.append(a if b is None else (b if a is None else jnp.maximum(a, b)))
        d = n // 2
        while d >= 1:
            for k in range(n):
                if k & d == 0:
                    w[k], w[k + d] = jnp.maximum(w[k], w[k + d]), jnp.minimum(w[k], w[k + d])
            d //= 2
        vs = w
    return vs


def _topk_kernel(h2t_ref, wq_ref, keys_ref, thr_ref, s2_ref, e1_ref, e2_ref, qt_ref):
    nk = PEER_N_KEYS
    ntb = TOPK_TOKENS // LANES
    nv = nk // SUBLANES
    qt_ref[...] = jnp.dot(wq_ref[...], h2t_ref[...], preferred_element_type=F32).astype(BF16)

    def pack(rep, sub_iota):
        out = rep[SUBLANES - 1]
        for r in range(SUBLANES - 2, -1, -1):
            out = jnp.where(sub_iota == r, rep[r], out)
        return out

    def head_body(hh, carry):
        sub_iota = lax.broadcasted_iota(jnp.int32, (SUBLANES, LANES), 0)
        sts = []
        for p in range(2):
            off = pl.multiple_of((hh * 2 + p) * nk, nk)
            sts.append(jnp.dot(keys_ref[hh * 2 + p], qt_ref[pl.ds(off, nk), :],
                               preferred_element_type=F32))
        for tb in range(ntb):
            lsl = slice(tb * LANES, (tb + 1) * LANES)
            rows = [[sts[p][r * SUBLANES:(r + 1) * SUBLANES, lsl] for r in range(nv)] for p in range(2)]
            a1 = _merge_top16_over_sublanes(_sort_desc(rows[0]))
            a2 = _merge_top16_over_sublanes(_sort_desc(rows[1]))
            a1lo, a1hi = pack(a1[:SUBLANES], sub_iota), pack(a1[SUBLANES:], sub_iota)
            a2lo, a2hi = pack(a2[:SUBLANES], sub_iota), pack(a2[SUBLANES:], sub_iota)
            tail = sub_iota >= 2
            cands = [a1lo + a2[0], a1hi + a2[0], a1lo + a2[1],
                     jnp.where(tail, a1[0] + a2lo, NEG_INF), a1[0] + a2hi,
                     jnp.where(tail, a1[1] + a2lo, NEG_INF)]
            cands += [jnp.where(tail, a1lo + a2[l], NEG_INF) for l in (2, 3, 4)]
            top = _merge_top16_over_sublanes(_sort_desc(cands))
            tau = top[PEER_TOPK - 1]
            z = functools.reduce(lambda x, y: x + y, [jnp.exp(t - top[0]) for t in top])
            rz = 1.0 / z
            for r in range(nv):
                rsl = slice(r * SUBLANES, (r + 1) * SUBLANES)
                s1v, s2v = rows[0][r], rows[1][r]
                thr = jnp.full((SUBLANES, LANES), jnp.inf, F32)
                for l in range(PEER_TOPK):
                    thr = jnp.where(s1v + a2[l] >= tau, a2[l], thr)
                thr_ref[tb, hh, rsl, :] = thr
                s2_ref[tb, hh, rsl, :] = s2v
                e1_ref[tb, hh, rsl, :] = jnp.exp(s1v - a1[0]) * rz
                e2_ref[tb, hh, rsl, :] = jnp.exp(s2v - a2[0])
        return carry

    lax.fori_loop(0, PEER_HEADS, head_body, 0)


def _topk(h2t, wq_t, keys):
    d, n = h2t.shape
    tt = TOPK_TOKENS
    ntb = tt // LANES
    nb = n // LANES
    shape = (nb, PEER_HEADS, PEER_N_KEYS, LANES)
    bigspec = pl.BlockSpec((ntb, PEER_HEADS, PEER_N_KEYS, LANES), lambda i: (i, 0, 0, 0))
    return pl.pallas_call(
        _topk_kernel,
        grid=(n // tt,),
        in_specs=[pl.BlockSpec((d, tt), lambda i: (0, i)),
                  pl.BlockSpec(wq_t.shape, lambda i: (0, 0)),
                  pl.BlockSpec(keys.shape, lambda i: (0, 0, 0))],
        out_specs=[bigspec, bigspec, bigspec, bigspec],
        out_shape=[jax.ShapeDtypeStruct(shape, F32)] * 4,
        scratch_shapes=[pltpu.VMEM((wq_t.shape[0], tt), BF16)],
        compiler_params=_cparams(("arbitrary",)),
        name="topk",
    )(h2t, wq_t, keys)


def _peer_kernel(h2t_ref, u_first_ref, u_ref, vt_ref, vt_last_ref,
                 thr_ref, s2_ref, e1_ref, e2_ref, x1_ref, g2_ref, fg_ref,
                 o_ref, acc_ref, st_ref, a_ref):
    e = pl.program_id(1)
    ntb = PEER_TOKENS // LANES
    nk = PEER_N_KEYS
    npass = PEER_GROUP // PEER_PASS
    ni = PEER_PASS // nk
    nrow = PEER_JC // SUBLANES
    nq = PEER_EXPERTS // PEER_GROUP

    @pl.when(e == 0)
    def _():
        acc_ref[...] = jnp.zeros_like(acc_ref)
        a_ref[1] = jnp.zeros((PEER_GROUP, PEER_TOKENS), BF16)
        st_ref[0] = jnp.dot(u_first_ref[...], h2t_ref[...], preferred_element_type=F32)

    def row8(ref, tb, hh, key):
        return jnp.broadcast_to(ref[tb, hh, pl.ds(key, 1), :], (SUBLANES, LANES))

    def gates(q, cur):
        for ps in range(npass):
            key0 = (e * nq + q) * (PEER_GROUP // nk) + ps * ni
            for tb in range(ntb):
                lsl = slice(tb * LANES, (tb + 1) * LANES)
                for jc in range(nk // PEER_JC):
                    gs = [[jnp.zeros((SUBLANES, LANES), F32) for _ in range(nrow)] for _ in range(ni)]
                    for hh in range(PEER_HEADS):
                        thrs = [row8(thr_ref, tb, hh, key0 + ii) for ii in range(ni)]
                        e1s = [row8(e1_ref, tb, hh, key0 + ii) for ii in range(ni)]
                        for r in range(nrow):
                            r0 = jc * PEER_JC + r * SUBLANES
                            s2v = s2_ref[tb, hh, r0:r0 + SUBLANES, :]
                            e2v = e2_ref[tb, hh, r0:r0 + SUBLANES, :]
                            for ii in range(ni):
                                gs[ii][r] = gs[ii][r] + jnp.where(s2v >= thrs[ii], e2v, 0.0) * e1s[ii]
                    for ii in range(ni):
                        r0 = ps * PEER_PASS + ii * nk + jc * PEER_JC
                        rsl = slice(r0, r0 + PEER_JC)
                        a = _gelu(st_ref[cur, rsl, lsl]) * jnp.concatenate(gs[ii], axis=0)
                        a_ref[cur, rsl, lsl] = a.astype(BF16)

    def group_body(q, carry):
        cur = jnp.bitwise_and(q, 1)
        nxt = 1 - cur
        acc_ref[...] += jnp.dot(vt_ref[q], a_ref[nxt], preferred_element_type=F32)
        gates(q, cur)
        row0 = pl.multiple_of(q * PEER_GROUP, PEER_GROUP)
        st_ref[nxt] = jnp.dot(u_ref[pl.ds(row0, PEER_GROUP), :], h2t_ref[...], preferred_element_type=F32)
        return carry

    lax.fori_loop(0, nq, group_body, 0)

    @pl.when(e == pl.num_programs(1) - 1)
    def _():
        acc_ref[...] += jnp.dot(vt_last_ref[0], a_ref[1], preferred_element_type=F32)
        x2 = x1_ref[...] + g2_ref[0] * acc_ref[...].T
        o_ref[...] = _rms(x2, fg_ref[...])


def _peer(h2t, u_rot, vt_rot, thr, s2, e1, e2, x1, mod3, fg, seq):
    d, n = h2t.shape
    tt = PEER_TOKENS
    eb = PEER_EXPERTS
    gq = PEER_GROUP
    nq = eb // gq
    ntb = tt // LANES
    tpb = seq // tt
    ngroups = u_rot.shape[0] // gq
    assert ngroups % 2 == 0 and nq % 2 == 0
    bigspec = pl.BlockSpec((ntb, PEER_HEADS, PEER_N_KEYS, LANES), lambda t, e: (t, 0, 0, 0))
    return pl.pallas_call(
        _peer_kernel,
        grid=(n // tt, ngroups // nq),
        in_specs=[pl.BlockSpec((d, tt), lambda t, e: (0, t)),
                  pl.BlockSpec((gq, d), lambda t, e: (ngroups - 1, 0)),
                  pl.BlockSpec((eb, d), lambda t, e: (e, 0)),
                  pl.BlockSpec((nq, d, gq), lambda t, e: (e, 0, 0)),
                  pl.BlockSpec((1, d, gq), lambda t, e: (0, 0, 0)),
                  bigspec, bigspec, bigspec, bigspec,
                  pl.BlockSpec((tt, d), lambda t, e: (t, 0)),
                  pl.BlockSpec((1, 1, d), lambda t, e: (t // tpb, 0, 5)),
                  pl.BlockSpec((1, d), lambda t, e: (0, 0))],
        out_specs=pl.BlockSpec((tt, d), lambda t, e: (t, 0)),
        out_shape=jax.ShapeDtypeStruct((n, d), F32),
        scratch_shapes=[pltpu.VMEM((d, tt), F32),
                        pltpu.VMEM((2, gq, tt), F32),
                        pltpu.VMEM((2, gq, tt), BF16)],
        compiler_params=_cparams(("arbitrary", "arbitrary")),
        name="peer",
    )(h2t, u_rot, u_rot, vt_rot, vt_rot, thr, s2, e1, e2, x1, mod3, fg)


def _retention_tables():
    hcount = RET_HEADS
    c = CHUNK
    gamma = 1.0 - 2.0 ** (-5.0 - jnp.arange(hcount, dtype=F32))
    log_g = jnp.log(gamma)
    idx = jnp.arange(c, dtype=F32)
    diff = idx[:, None] - idx[None, :]
    din = jnp.where((diff >= 0)[None], jnp.exp(log_g[:, None, None] * jnp.maximum(diff, 0.0)[None]), 0.0)
    dq = jnp.exp(log_g[:, None] * (idx[None, :] + 1.0))
    dk = jnp.exp(log_g[:, None] * (c - 1.0 - idx[None, :]))
    dc = jnp.exp(log_g * c)
    dq = jnp.broadcast_to(dq[:, :, None], (hcount, c, RET_V_DIM))
    dk = jnp.broadcast_to(dk[:, :, None], (hcount, c, RET_QK_DIM))
    dc = jnp.broadcast_to(dc[:, None, None], (hcount, RET_QK_DIM, RET_V_DIM))
    return din.astype(F32), dq.astype(F32), dk.astype(F32), dc.astype(F32)


def _rotary_tables(seq):
    half = RET_QK_DIM // 2
    pos = jnp.arange(seq, dtype=F32)
    inv = ROPE_BASE ** (-jnp.arange(half, dtype=F32) * 2.0 / RET_QK_DIM)
    ang = pos[:, None] * inv[None, :]
    cos = jnp.cos(ang)
    sin = jnp.sin(ang)
    return jnp.concatenate([cos, cos], axis=-1), jnp.concatenate([-sin, sin], axis=-1)


def kernel(x, c, w_ada, b_ada, norm1_g, w_in, ret_gn_g, sgu_ln_g, sgu_ln_b, sgu_w, sgu_b, w_ret_out,
           w_sgu_out, w_out, norm2_g, peer_w_q, peer_sub_keys, peer_u, peer_v, final_g):
    bsz, seq, d = x.shape
    n = bsz * seq
    depth = w_ada.shape[0]
    assert d == D_MODEL and seq % TM_PROJ == 0 and seq % PEER_TOKENS == 0 and seq % TOPK_TOKENS == 0
    cos_t, sin_t = _rotary_tables(seq)
    din, dq, dk, dc = _retention_tables()
    xc = x.reshape(n, d)
    for l in range(depth):
        mod3 = _ada(c, w_ada[l], b_ada[l]).reshape(bsz, 1, N_MOD * d)
        proj = _inproj(xc, mod3, norm1_g[l].reshape(1, d), cos_t, sin_t,
                       sgu_ln_g[l].reshape(1, d), sgu_ln_b[l].reshape(1, d), w_in[l].astype(BF16), seq)
        bs = jnp.broadcast_to(sgu_b[l][:, :, None], (SGU_GROUPS, CHUNK, SGU_GROUP_DIM))
        ya, ys = _chunk(proj, din, dq, dk, dc, ret_gn_g[l].reshape(1, d), sgu_w[l], bs, bsz, seq)
        x1, h2t = _merge(ya, ys, proj, xc, mod3, norm2_g[l].reshape(1, d), w_ret_out[l].astype(BF16),
                        w_sgu_out[l].astype(BF16), w_out[l].astype(BF16), seq)
        wq_t = peer_w_q[l].T.astype(BF16)
        keys = peer_sub_keys[l].reshape(PEER_HEADS * 2, PEER_N_KEYS, -1).astype(BF16)
        thr, s2, e1, e2 = _topk(h2t, wq_t, keys)
        assert depth == 1
        u_rot = jnp.roll(peer_u[l].astype(BF16), -PEER_GROUP, axis=0)
        vt_rot = jnp.roll(peer_v[l].astype(BF16).reshape(-1, PEER_GROUP, d).transpose(0, 2, 1), 1, axis=0)
        xc = _peer(h2t, u_rot, vt_rot, thr, s2, e1, e2, x1, mod3, final_g.reshape(1, d), seq)
    return xc.reshape(bsz, seq, d)
```

```python
import functools
import math

import jax
import jax.numpy as jnp
from jax import lax
from jax.experimental import pallas as pl
from jax.experimental.pallas import tpu as pltpu

F32 = jnp.float32
BF16 = jnp.bfloat16

D_MODEL = 1024
RET_HEADS = 4
RET_QK_DIM = 128
RET_V_DIM = 256
CHUNK = 128
SGU_GROUPS = 4
SGU_GROUP_DIM = 256
IN_WIDTH = 7168
PEER_HEADS = 8
PEER_N_KEYS = 128
PEER_TOPK = 16
N_MOD = 6
EPS = 1e-6
ROPE_BASE = 10000.0
NEG_INF = float("-inf")

VMEM_LIMIT_BYTES = 56 * 1024 * 1024

TM_PROJ = 512
TOPK_TOKENS = 256
PEER_TOKENS = 512
PEER_EXPERTS = 1024
PEER_GROUP = 512
PEER_PASS = 256
LANES = 128
SUBLANES = 8
PACKED_ROWS = 16


def _gelu(x):
    return 0.5 * x * (1.0 + lax.erf(x * (1.0 / math.sqrt(2.0))))


def _rms(x, g):
    ms = jnp.mean(x * x, axis=-1, keepdims=True)
    return x * lax.rsqrt(ms + EPS) * g


def _cparams(sem):
    return pltpu.CompilerParams(dimension_semantics=sem, vmem_limit_bytes=VMEM_LIMIT_BYTES)


def _ada_kernel(c_ref, w_ref, b_ref, o_ref):
    c = c_ref[...]
    ca = c * jax.nn.sigmoid(c)
    o_ref[...] = jnp.dot(ca.astype(BF16), w_ref[...].astype(BF16),
                         preferred_element_type=F32) + b_ref[...]


def _ada(c, w, b):
    bsz, d = c.shape
    n_out = w.shape[1]
    return pl.pallas_call(
        _ada_kernel,
        grid=(n_out // d,),
        in_specs=[pl.BlockSpec((bsz, d), lambda j: (0, 0)),
                  pl.BlockSpec((d, d), lambda j: (0, j)),
                  pl.BlockSpec((1, d), lambda j: (0, j))],
        out_specs=pl.BlockSpec((bsz, d), lambda j: (0, j)),
        out_shape=jax.ShapeDtypeStruct((bsz, n_out), F32),
        compiler_params=_cparams(("arbitrary",)),
        name="ada",
    )(c, w, b.reshape(1, n_out))


def _inproj_kernel(x_ref, sh_ref, sc_ref, g_ref, cos_ref, sin_ref, lng_ref, lnb_ref, w_ref, o_ref):
    h = (_rms(x_ref[...], g_ref[...]) * (1.0 + sc_ref[0]) + sh_ref[0]).astype(BF16)
    d = D_MODEL
    for j in range(IN_WIDTH // d):
        acc = jnp.dot(h, w_ref[:, j * d:(j + 1) * d], preferred_element_type=F32)
        if j == 0:
            cos = cos_ref[...]
            sin = sin_ref[...]
            for hh in range(2 * RET_HEADS):
                blk = acc[:, hh * RET_QK_DIM:(hh + 1) * RET_QK_DIM]
                r = blk * cos + pltpu.roll(blk, RET_QK_DIM // 2, 1) * sin
                if hh >= RET_HEADS:
                    r = r * (RET_QK_DIM ** -0.5)
                o_ref[:, hh * RET_QK_DIM:(hh + 1) * RET_QK_DIM] = r.astype(BF16)
            continue
        if j == 1:
            res = acc
        elif j == 2:
            res = acc * jax.nn.sigmoid(acc)
        elif j == 3:
            res = _gelu(acc)
        elif j == 4:
            a = _gelu(acc)
            mu = jnp.mean(a, axis=-1, keepdims=True)
            ac = a - mu
            var = jnp.mean(ac * ac, axis=-1, keepdims=True)
            res = ac * lax.rsqrt(var + EPS) * lng_ref[...] + lnb_ref[...]
        else:
            res = jax.nn.sigmoid(acc)
        o_ref[:, j * d:(j + 1) * d] = res.astype(BF16)


def _inproj(x2d, mod3, g, cos_t, sin_t, ln_g, ln_b, w_bf, seq):
    n, d = x2d.shape
    tm = TM_PROJ
    tpb = seq // tm
    return pl.pallas_call(
        _inproj_kernel,
        grid=(n // tm,),
        in_specs=[pl.BlockSpec((tm, d), lambda i: (i, 0)),
                  pl.BlockSpec((1, 1, d), lambda i: (i // tpb, 0, 0)),
                  pl.BlockSpec((1, 1, d), lambda i: (i // tpb, 0, 1)),
                  pl.BlockSpec((1, d), lambda i: (0, 0)),
                  pl.BlockSpec((tm, RET_QK_DIM), lambda i: (i % tpb, 0)),
                  pl.BlockSpec((tm, RET_QK_DIM), lambda i: (i % tpb, 0)),
                  pl.BlockSpec((1, d), lambda i: (0, 0)),
                  pl.BlockSpec((1, d), lambda i: (0, 0)),
                  pl.BlockSpec((d, IN_WIDTH), lambda i: (0, 0), pipeline_mode=pl.Buffered(1))],
        out_specs=pl.BlockSpec((tm, IN_WIDTH), lambda i: (i, 0)),
        out_shape=jax.ShapeDtypeStruct((n, IN_WIDTH), BF16),
        compiler_params=_cparams(("arbitrary",)),
        name="inproj",
    )(x2d, mod3, mod3, g, cos_t, sin_t, ln_g, ln_b, w_bf)


def _chunk_kernel(qk_ref, v_ref, gs_ref, u_ref, vn_ref, din_ref, dq_ref, dk_ref, dc_ref,
                  gn_ref, ws_ref, bs_ref, ya_ref, ys_ref, state_ref):
    @pl.when(pl.program_id(1) == 0)
    def _():
        state_ref[...] = jnp.zeros_like(state_ref)

    for hh in range(RET_HEADS):
        q = qk_ref[:, hh * RET_QK_DIM:(hh + 1) * RET_QK_DIM]
        k = qk_ref[:, (RET_HEADS + hh) * RET_QK_DIM:(RET_HEADS + hh + 1) * RET_QK_DIM]
        v = v_ref[:, hh * RET_V_DIM:(hh + 1) * RET_V_DIM]
        state = state_ref[hh]
        s = lax.dot_general(q, k, (((1,), (1,)), ((), ())), preferred_element_type=F32) * din_ref[hh]
        inner = jnp.dot(s.astype(BF16), v, preferred_element_type=F32)
        cross = jnp.dot(q, state.astype(BF16), preferred_element_type=F32) * dq_ref[hh]
        kd_t = (k.astype(F32) * dk_ref[hh]).T.astype(BF16)
        state_ref[hh] = state * dc_ref[hh] + jnp.dot(kd_t, v, preferred_element_type=F32)
        y = inner + cross
        mu = jnp.mean(y, axis=-1, keepdims=True)
        yc = y - mu
        var = jnp.mean(yc * yc, axis=-1, keepdims=True)
        sl = slice(hh * RET_V_DIM, (hh + 1) * RET_V_DIM)
        yn = yc * lax.rsqrt(var + EPS) * gn_ref[:, sl]
        ya_ref[:, sl] = (gs_ref[:, sl].astype(F32) * yn).astype(BF16)

    row = lax.broadcasted_iota(jnp.int32, (CHUNK, CHUNK), 0)
    col = lax.broadcasted_iota(jnp.int32, (CHUNK, CHUNK), 1)
    for gg in range(SGU_GROUPS):
        sl = slice(gg * SGU_GROUP_DIM, (gg + 1) * SGU_GROUP_DIM)
        ws = jnp.where(row >= col, ws_ref[gg], 0.0).astype(BF16)
        mixed = jnp.dot(ws, vn_ref[:, sl], preferred_element_type=F32) + bs_ref[gg]
        ys_ref[:, sl] = (u_ref[:, sl].astype(F32) * mixed).astype(BF16)


def _chunk(proj, din, dq, dk, dc, gn, ws, bs, bsz, seq):
    n = proj.shape[0]
    nc = seq // CHUNK
    d = D_MODEL

    def col(j):
        return pl.BlockSpec((CHUNK, d), lambda b, c: (b * nc + c, j))

    def const(shape):
        return pl.BlockSpec(shape, lambda b, c: (0,) * len(shape))

    return pl.pallas_call(
        _chunk_kernel,
        grid=(bsz, nc),
        in_specs=[col(0), col(1), col(2), col(3), col(4),
                  const(din.shape), const(dq.shape), const(dk.shape), const(dc.shape),
                  const(gn.shape), const(ws.shape), const(bs.shape)],
        out_specs=[pl.BlockSpec((CHUNK, d), lambda b, c: (b * nc + c, 0)),
                   pl.BlockSpec((CHUNK, d), lambda b, c: (b * nc + c, 0))],
        out_shape=[jax.ShapeDtypeStruct((n, d), BF16), jax.ShapeDtypeStruct((n, d), BF16)],
        scratch_shapes=[pltpu.VMEM((RET_HEADS, RET_QK_DIM, RET_V_DIM), F32)],
        compiler_params=_cparams(("arbitrary", "arbitrary")),
        name="chunk",
    )(proj, proj, proj, proj, proj, din, dq, dk, dc, gn, ws, bs)


def _merge_kernel(ya_ref, ys_ref, sa_ref, sb_ref, x_ref, g1_ref, sh2_ref, sc2_ref, n2_ref,
                  wa_ref, wb_ref, wo_ref, x1_ref, h2t_ref):
    ba = jnp.dot(ya_ref[...], wa_ref[...], preferred_element_type=F32)
    bb = jnp.dot(ys_ref[...], wb_ref[...], preferred_element_type=F32)
    merged = sa_ref[...].astype(F32) * ba + sb_ref[...].astype(F32) * bb
    mo = jnp.dot(merged.astype(BF16), wo_ref[...], preferred_element_type=F32)
    x1 = x_ref[...] + g1_ref[0] * mo
    x1_ref[...] = x1
    h2t_ref[...] = (_rms(x1, n2_ref[...]) * (1.0 + sc2_ref[0]) + sh2_ref[0]).T.astype(BF16)


def _merge(ya, ys, proj, x2d, mod3, n2, wa, wb, wo, seq):
    n, d = x2d.shape
    tm = TM_PROJ
    tpb = seq // tm

    def modspec(k):
        return pl.BlockSpec((1, 1, d), lambda i: (i // tpb, 0, k))

    def wspec():
        return pl.BlockSpec((d, d), lambda i: (0, 0))

    return pl.pallas_call(
        _merge_kernel,
        grid=(n // tm,),
        in_specs=[pl.BlockSpec((tm, d), lambda i: (i, 0)),
                  pl.BlockSpec((tm, d), lambda i: (i, 0)),
                  pl.BlockSpec((tm, d), lambda i: (i, 5)),
                  pl.BlockSpec((tm, d), lambda i: (i, 6)),
                  pl.BlockSpec((tm, d), lambda i: (i, 0)),
                  modspec(2), modspec(3), modspec(4),
                  pl.BlockSpec((1, d), lambda i: (0, 0)),
                  wspec(), wspec(), wspec()],
        out_specs=[pl.BlockSpec((tm, d), lambda i: (i, 0)),
                   pl.BlockSpec((d, tm), lambda i: (0, i))],
        out_shape=[jax.ShapeDtypeStruct((n, d), F32), jax.ShapeDtypeStruct((d, n), BF16)],
        compiler_params=_cparams(("arbitrary",)),
        name="merge",
    )(ya, ys, proj, proj, x2d, mod3, mod3, mod3, n2, wa, wb, wo)


def _sort_network(n):
    pairs = []
    p = 1
    while p < n:
        k = p
        while k >= 1:
            for j in range(k % p, n - k, 2 * k):
                for i in range(min(k, n - j - k)):
                    if (i + j) // (2 * p) == (i + j + k) // (2 * p):
                        pairs.append((i + j, i + j + k))
            k //= 2
        p *= 2
    return pairs


def _sort_desc(vs):
    vs = list(vs)
    for i, j in _sort_network(PEER_TOPK):
        if j < len(vs):
            vs[i], vs[j] = jnp.maximum(vs[i], vs[j]), jnp.minimum(vs[i], vs[j])
    return vs


def _merge_top16_over_sublanes(vs):
    n = PEER_TOPK
    vs = list(vs) + [None] * (n - len(vs))
    for shift in (4, 2, 1):
        w = []
        for k in range(n):
            a, b = vs[k], vs[n - 1 - k]
            b = None if b is None else pltpu.roll(b, shift, 0)
            w.append(a if b is None else (b if a is None else jnp.maximum(a, b)))
        d = n // 2
        while d >= 1:
            for k in range(n):
                if k & d == 0:
                    w[k], w[k + d] = jnp.maximum(w[k], w[k + d]), jnp.minimum(w[k], w[k + d])
            d //= 2
        vs = w
    return vs


def _topk_kernel(h2t_ref, wq_ref, keys_ref, cnt_ref, e1_ref, rank_ref, e2_ref, qt_ref):
    nk = PEER_N_KEYS
    ntb = TOPK_TOKENS // LANES
    nv = nk // SUBLANES
    qt_ref[...] = jnp.dot(wq_ref[...], h2t_ref[...], preferred_element_type=F32).astype(BF16)

    def pack(rep, sub_iota):
        out = rep[SUBLANES - 1]
        for r in range(SUBLANES - 2, -1, -1):
            out = jnp.where(sub_iota == r, rep[r], out)
        return out

    def head_body(hh, carry):
        sub_iota = lax.broadcasted_iota(jnp.int32, (SUBLANES, LANES), 0)
        sts = []
        for p in range(2):
            off = pl.multiple_of((hh * 2 + p) * nk, nk)
            sts.append(jnp.dot(keys_ref[hh * 2 + p], qt_ref[pl.ds(off, nk), :],
                               preferred_element_type=F32))
        for tb in range(ntb):
            lsl = slice(tb * LANES, (tb + 1) * LANES)
            rows = [[sts[p][r * SUBLANES:(r + 1) * SUBLANES, lsl] for r in range(nv)] for p in range(2)]
            a1 = _merge_top16_over_sublanes(_sort_desc(rows[0]))
            a2 = _merge_top16_over_sublanes(_sort_desc(rows[1]))
            a1lo, a1hi = pack(a1[:SUBLANES], sub_iota), pack(a1[SUBLANES:], sub_iota)
            a2lo, a2hi = pack(a2[:SUBLANES], sub_iota), pack(a2[SUBLANES:], sub_iota)
            tail = sub_iota >= 2
            cands = [a1lo + a2[0], a1hi + a2[0], a1lo + a2[1],
                     jnp.where(tail, a1[0] + a2lo, NEG_INF), a1[0] + a2hi,
                     jnp.where(tail, a1[1] + a2lo, NEG_INF)]
            cands += [jnp.where(tail, a1lo + a2[l], NEG_INF) for l in (2, 3, 4)]
            top = _merge_top16_over_sublanes(_sort_desc(cands))
            tau = top[PEER_TOPK - 1]
            z = functools.reduce(lambda x, y: x + y, [jnp.exp(t - top[0]) for t in top])
            rz = 1.0 / z
            for rp in range(nv // 2):
                ranks, e2s = [], []
                for r in (2 * rp, 2 * rp + 1):
                    rsl = slice(r * SUBLANES, (r + 1) * SUBLANES)
                    s1v, s2v = rows[0][r], rows[1][r]
                    cnt = jnp.zeros((SUBLANES, LANES), F32)
                    rank = jnp.zeros((SUBLANES, LANES), F32)
                    for l in range(PEER_TOPK):
                        cnt = jnp.where(s1v + a2[l] >= tau, l + 1.0, cnt)
                        rank = jnp.where(a2[l] > s2v, l + 1.0, rank)
                    cnt_ref[tb, hh, rsl, :] = cnt
                    e1_ref[tb, hh, rsl, :] = jnp.exp(s1v - a1[0]) * rz
                    ranks.append(rank)
                    e2s.append(jnp.exp(s2v - a2[0]))
                psl = slice(rp * SUBLANES, (rp + 1) * SUBLANES)
                rank_ref[tb, hh, psl, :] = pltpu.bitcast(jnp.concatenate(ranks, axis=0).astype(BF16), jnp.uint32)
                e2_ref[tb, hh, psl, :] = pltpu.bitcast(jnp.concatenate(e2s, axis=0).astype(BF16), jnp.uint32)
        return carry

    lax.fori_loop(0, PEER_HEADS, head_body, 0)


def _topk(h2t, wq_t, keys):
    d, n = h2t.shape
    tt = TOPK_TOKENS
    ntb = tt // LANES
    nb = n // LANES
    shape = (nb, PEER_HEADS, PEER_N_KEYS, LANES)
    pshape = (nb, PEER_HEADS, PEER_N_KEYS // 2, LANES)
    bigspec = pl.BlockSpec((ntb,) + shape[1:], lambda i: (i, 0, 0, 0))
    pspec = pl.BlockSpec((ntb,) + pshape[1:], lambda i: (i, 0, 0, 0))
    return pl.pallas_call(
        _topk_kernel,
        grid=(n // tt,),
        in_specs=[pl.BlockSpec((d, tt), lambda i: (0, i)),
                  pl.BlockSpec(wq_t.shape, lambda i: (0, 0)),
                  pl.BlockSpec(keys.shape, lambda i: (0, 0, 0))],
        out_specs=[bigspec, bigspec, pspec, pspec],
        out_shape=[jax.ShapeDtypeStruct(shape, F32), jax.ShapeDtypeStruct(shape, F32),
                   jax.ShapeDtypeStruct(pshape, jnp.uint32), jax.ShapeDtypeStruct(pshape, jnp.uint32)],
        scratch_shapes=[pltpu.VMEM((wq_t.shape[0], tt), BF16)],
        compiler_params=_cparams(("arbitrary",)),
        name="topk",
    )(h2t, wq_t, keys)


def _peer_kernel(h2t_ref, u_ref, vt_ref, cnt_ref, e1_ref, rank_ref, e2_ref, x1_ref, g2_ref, fg_ref,
                 o_ref, acc_ref, st_ref, a_ref):
    e = pl.program_id(1)
    ntb = PEER_TOKENS // LANES
    nk = PEER_N_KEYS
    npass = PEER_GROUP // PEER_PASS
    ni = PEER_PASS // nk
    nrow = nk // PACKED_ROWS
    nq = PEER_EXPERTS // PEER_GROUP

    @pl.when(e == 0)
    def _():
        acc_ref[...] = jnp.zeros_like(acc_ref)

    def packed_row(ref, tb, hh, key):
        return jnp.broadcast_to(ref[tb, hh, pl.ds(key, 1), :], (PACKED_ROWS, LANES)).astype(BF16)

    def gates(g):
        def pass_body(ps, carry):
            key0 = (e * nq + g) * (PEER_GROUP // nk) + ps * ni
            rbase = pl.multiple_of(ps * PEER_PASS, PEER_PASS)
            for tb in range(ntb):
                lsl = slice(tb * LANES, (tb + 1) * LANES)
                gs = [[jnp.zeros((PACKED_ROWS, LANES), BF16) for _ in range(nrow)] for _ in range(ni)]
                for hh in range(PEER_HEADS):
                    cnts = [packed_row(cnt_ref, tb, hh, key0 + ii) for ii in range(ni)]
                    e1s = [packed_row(e1_ref, tb, hh, key0 + ii) for ii in range(ni)]
                    for r in range(nrow):
                        wsl = slice(r * SUBLANES, (r + 1) * SUBLANES)
                        rk = pltpu.bitcast(rank_ref[tb, hh, wsl, :], BF16)
                        e2v = pltpu.bitcast(e2_ref[tb, hh, wsl, :], BF16)
                        for ii in range(ni):
                            gs[ii][r] = gs[ii][r] + jnp.where(rk < cnts[ii], e2v, jnp.zeros_like(e2v)) * e1s[ii]
                for ii in range(ni):
                    for r in range(nrow):
                        rows = pl.ds(rbase + ii * nk + r * PACKED_ROWS, PACKED_ROWS)
                        a_ref[rows, lsl] = _gelu(st_ref[rows, lsl]).astype(BF16) * gs[ii][r]
            return carry

        lax.fori_loop(0, npass, pass_body, 0)

    for g in range(nq):
        gsl = slice(g * PEER_GROUP, (g + 1) * PEER_GROUP)
        st_ref[...] = jnp.dot(u_ref[gsl, :], h2t_ref[...], preferred_element_type=F32)
        gates(g)
        acc_ref[...] += jnp.dot(vt_ref[:, gsl], a_ref[...], preferred_element_type=F32)

    @pl.when(e == pl.num_programs(1) - 1)
    def _():
        x2 = x1_ref[...] + g2_ref[0] * acc_ref[...].T
        o_ref[...] = _rms(x2, fg_ref[...])


def _peer(h2t, u_bf, vt_bf, cnt, e1, rank, e2, x1, mod3, fg, seq):
    d, n = h2t.shape
    tt = PEER_TOKENS
    eb = PEER_EXPERTS
    gq = PEER_GROUP
    ntb = tt // LANES
    tpb = seq // tt
    n_exp = u_bf.shape[0]
    bigspec = pl.BlockSpec((ntb, PEER_HEADS, PEER_N_KEYS, LANES), lambda t, e: (t, 0, 0, 0))
    pspec = pl.BlockSpec((ntb, PEER_HEADS, PEER_N_KEYS // 2, LANES), lambda t, e: (t, 0, 0, 0))
    return pl.pallas_call(
        _peer_kernel,
        grid=(n // tt, n_exp // eb),
        in_specs=[pl.BlockSpec((d, tt), lambda t, e: (0, t)),
                  pl.BlockSpec((eb, d), lambda t, e: (e, 0)),
                  pl.BlockSpec((d, eb), lambda t, e: (0, e)),
                  bigspec, bigspec, pspec, pspec,
                  pl.BlockSpec((tt, d), lambda t, e: (t, 0)),
                  pl.BlockSpec((1, 1, d), lambda t, e: (t // tpb, 0, 5)),
                  pl.BlockSpec((1, d), lambda t, e: (0, 0))],
        out_specs=pl.BlockSpec((tt, d), lambda t, e: (t, 0)),
        out_shape=jax.ShapeDtypeStruct((n, d), F32),
        scratch_shapes=[pltpu.VMEM((d, tt), F32),
                        pltpu.VMEM((gq, tt), F32),
                        pltpu.VMEM((gq, tt), BF16)],
        compiler_params=_cparams(("arbitrary", "arbitrary")),
        name="peer",
    )(h2t, u_bf, vt_bf, cnt, e1, rank, e2, x1, mod3, fg)


def _retention_tables():
    hcount = RET_HEADS
    c = CHUNK
    gamma = 1.0 - 2.0 ** (-5.0 - jnp.arange(hcount, dtype=F32))
    log_g = jnp.log(gamma)
    idx = jnp.arange(c, dtype=F32)
    diff = idx[:, None] - idx[None, :]
    din = jnp.where((diff >= 0)[None], jnp.exp(log_g[:, None, None] * jnp.maximum(diff, 0.0)[None]), 0.0)
    dq = jnp.exp(log_g[:, None] * (idx[None, :] + 1.0))
    dk = jnp.exp(log_g[:, None] * (c - 1.0 - idx[None, :]))
    dc = jnp.exp(log_g * c)
    dq = jnp.broadcast_to(dq[:, :, None], (hcount, c, RET_V_DIM))
    dk = jnp.broadcast_to(dk[:, :, None], (hcount, c, RET_QK_DIM))
    dc = jnp.broadcast_to(dc[:, None, None], (hcount, RET_QK_DIM, RET_V_DIM))
    return din.astype(F32), dq.astype(F32), dk.astype(F32), dc.astype(F32)


def _rotary_tables(seq):
    half = RET_QK_DIM // 2
    pos = jnp.arange(seq, dtype=F32)
    inv = ROPE_BASE ** (-jnp.arange(half, dtype=F32) * 2.0 / RET_QK_DIM)
    ang = pos[:, None] * inv[None, :]
    cos = jnp.cos(ang)
    sin = jnp.sin(ang)
    return jnp.concatenate([cos, cos], axis=-1), jnp.concatenate([-sin, sin], axis=-1)


def kernel(x, c, w_ada, b_ada, norm1_g, w_in, ret_gn_g, sgu_ln_g, sgu_ln_b, sgu_w, sgu_b, w_ret_out,
           w_sgu_out, w_out, norm2_g, peer_w_q, peer_sub_keys, peer_u, peer_v, final_g):
    bsz, seq, d = x.shape
    n = bsz * seq
    depth = w_ada.shape[0]
    assert d == D_MODEL and seq % TM_PROJ == 0 and seq % PEER_TOKENS == 0 and seq % TOPK_TOKENS == 0
    cos_t, sin_t = _rotary_tables(seq)
    din, dq, dk, dc = _retention_tables()
    xc = x.reshape(n, d)
    for l in range(depth):
        mod3 = _ada(c, w_ada[l], b_ada[l]).reshape(bsz, 1, N_MOD * d)
        proj = _inproj(xc, mod3, norm1_g[l].reshape(1, d), cos_t, sin_t,
                       sgu_ln_g[l].reshape(1, d), sgu_ln_b[l].reshape(1, d), w_in[l].astype(BF16), seq)
        bs = jnp.broadcast_to(sgu_b[l][:, :, None], (SGU_GROUPS, CHUNK, SGU_GROUP_DIM))
        ya, ys = _chunk(proj, din, dq, dk, dc, ret_gn_g[l].reshape(1, d), sgu_w[l], bs, bsz, seq)
        x1, h2t = _merge(ya, ys, proj, xc, mod3, norm2_g[l].reshape(1, d), w_ret_out[l].astype(BF16),
                        w_sgu_out[l].astype(BF16), w_out[l].astype(BF16), seq)
        wq_t = peer_w_q[l].T.astype(BF16)
        keys = peer_sub_keys[l].reshape(PEER_HEADS * 2, PEER_N_KEYS, -1).astype(BF16)
        cnt, e1, rank, e2 = _topk(h2t, wq_t, keys)
        assert depth == 1
        xc = _peer(h2t, peer_u[l].astype(BF16), peer_v[l].T.astype(BF16), cnt, e1, rank, e2, x1, mod3,
                   final_g.reshape(1, d), seq)
    return xc.reshape(bsz, seq, d)
```

```python
import functools
import math

import jax
import jax.numpy as jnp
from jax import lax
from jax.experimental import pallas as pl
from jax.experimental.pallas import tpu as pltpu

F32 = jnp.float32
BF16 = jnp.bfloat16

D_MODEL = 1024
RET_HEADS = 4
RET_QK_DIM = 128
RET_V_DIM = 256
CHUNK = 128
SGU_GROUPS = 4
SGU_GROUP_DIM = 256
IN_WIDTH = 7168
PEER_HEADS = 8
PEER_N_KEYS = 128
PEER_TOPK = 16
N_MOD = 6
EPS = 1e-6
ROPE_BASE = 10000.0
NEG_INF = float("-inf")

VMEM_LIMIT_BYTES = 56 * 1024 * 1024

TM_PROJ = 512
CHUNK_BATCHES = 4
TOPK_TOKENS = 512
PEER_TOKENS = 512
PEER_EXPERTS = 1024
PEER_GROUP = 1024
PEER_PASS = 256
LANES = 128
SUBLANES = 8
PACKED_ROWS = 16


def _gelu(x):
    return 0.5 * x * (1.0 + lax.erf(x * (1.0 / math.sqrt(2.0))))


def _rms(x, g):
    ms = jnp.mean(x * x, axis=-1, keepdims=True)
    return x * lax.rsqrt(ms + EPS) * g


def _cparams(sem):
    return pltpu.CompilerParams(dimension_semantics=sem, vmem_limit_bytes=VMEM_LIMIT_BYTES)


def _ada_kernel(c_ref, w_ref, b_ref, o_ref):
    c = c_ref[...]
    ca = c * jax.nn.sigmoid(c)
    o_ref[...] = jnp.dot(ca.astype(BF16), w_ref[...].astype(BF16),
                         preferred_element_type=F32) + b_ref[...]


def _ada(c, w, b):
    bsz, d = c.shape
    n_out = w.shape[1]
    return pl.pallas_call(
        _ada_kernel,
        grid=(n_out // d,),
        in_specs=[pl.BlockSpec((bsz, d), lambda j: (0, 0)),
                  pl.BlockSpec((d, d), lambda j: (0, j)),
                  pl.BlockSpec((1, d), lambda j: (0, j))],
        out_specs=pl.BlockSpec((bsz, d), lambda j: (0, j)),
        out_shape=jax.ShapeDtypeStruct((bsz, n_out), F32),
        compiler_params=_cparams(("arbitrary",)),
        name="ada",
    )(c, w, b.reshape(1, n_out))


def _inproj_kernel(x_ref, sh_ref, sc_ref, g_ref, cos_ref, sin_ref, lng_ref, lnb_ref, w_ref, o_ref):
    h = (_rms(x_ref[...], g_ref[...]) * (1.0 + sc_ref[0]) + sh_ref[0]).astype(BF16)
    d = D_MODEL
    for j in range(IN_WIDTH // d):
        acc = jnp.dot(h, w_ref[:, j * d:(j + 1) * d], preferred_element_type=F32)
        if j == 0:
            cos = cos_ref[...]
            sin = sin_ref[...]
            for hh in range(2 * RET_HEADS):
                blk = acc[:, hh * RET_QK_DIM:(hh + 1) * RET_QK_DIM]
                r = blk * cos + pltpu.roll(blk, RET_QK_DIM // 2, 1) * sin
                if hh >= RET_HEADS:
                    r = r * (RET_QK_DIM ** -0.5)
                o_ref[:, hh * RET_QK_DIM:(hh + 1) * RET_QK_DIM] = r.astype(BF16)
            continue
        if j == 1:
            res = acc
        elif j == 2:
            res = acc * jax.nn.sigmoid(acc)
        elif j == 3:
            res = _gelu(acc)
        elif j == 4:
            a = _gelu(acc)
            mu = jnp.mean(a, axis=-1, keepdims=True)
            ac = a - mu
            var = jnp.mean(ac * ac, axis=-1, keepdims=True)
            res = ac * lax.rsqrt(var + EPS) * lng_ref[...] + lnb_ref[...]
        else:
            res = jax.nn.sigmoid(acc)
        o_ref[:, j * d:(j + 1) * d] = res.astype(BF16)


def _inproj(x2d, mod3, g, cos_t, sin_t, ln_g, ln_b, w_bf, seq):
    n, d = x2d.shape
    tm = TM_PROJ
    tpb = seq // tm
    return pl.pallas_call(
        _inproj_kernel,
        grid=(n // tm,),
        in_specs=[pl.BlockSpec((tm, d), lambda i: (i, 0)),
                  pl.BlockSpec((1, 1, d), lambda i: (i // tpb, 0, 0)),
                  pl.BlockSpec((1, 1, d), lambda i: (i // tpb, 0, 1)),
                  pl.BlockSpec((1, d), lambda i: (0, 0)),
                  pl.BlockSpec((tm, RET_QK_DIM), lambda i: (i % tpb, 0)),
                  pl.BlockSpec((tm, RET_QK_DIM), lambda i: (i % tpb, 0)),
                  pl.BlockSpec((1, d), lambda i: (0, 0)),
                  pl.BlockSpec((1, d), lambda i: (0, 0)),
                  pl.BlockSpec((d, IN_WIDTH), lambda i: (0, 0), pipeline_mode=pl.Buffered(1))],
        out_specs=pl.BlockSpec((tm, IN_WIDTH), lambda i: (i, 0)),
        out_shape=jax.ShapeDtypeStruct((n, IN_WIDTH), BF16),
        compiler_params=_cparams(("arbitrary",)),
        name="inproj",
    )(x2d, mod3, mod3, g, cos_t, sin_t, ln_g, ln_b, w_bf)


def _chunk_kernel(qk_ref, v_ref, gs_ref, u_ref, vn_ref, din_ref, dq_ref, dk_ref, dc_ref,
                  gn_ref, ws_ref, bs_ref, ya_ref, ys_ref, state_ref):
    @pl.when(pl.program_id(1) == 0)
    def _():
        state_ref[...] = jnp.zeros_like(state_ref)

    row = lax.broadcasted_iota(jnp.int32, (CHUNK, CHUNK), 0)
    col = lax.broadcasted_iota(jnp.int32, (CHUNK, CHUNK), 1)
    ws = [jnp.where(row >= col, ws_ref[gg], 0.0).astype(BF16) for gg in range(SGU_GROUPS)]

    for b in range(qk_ref.shape[0]):
        for hh in range(RET_HEADS):
            q = qk_ref[b, :, hh * RET_QK_DIM:(hh + 1) * RET_QK_DIM]
            k = qk_ref[b, :, (RET_HEADS + hh) * RET_QK_DIM:(RET_HEADS + hh + 1) * RET_QK_DIM]
            v = v_ref[b, :, hh * RET_V_DIM:(hh + 1) * RET_V_DIM]
            state = state_ref[b, hh]
            s = lax.dot_general(q, k, (((1,), (1,)), ((), ())), preferred_element_type=F32) * din_ref[hh]
            inner = jnp.dot(s.astype(BF16), v, preferred_element_type=F32)
            cross = jnp.dot(q, state.astype(BF16), preferred_element_type=F32) * dq_ref[hh]
            kd_t = (k.astype(F32) * dk_ref[hh]).T.astype(BF16)
            state_ref[b, hh] = state * dc_ref[hh] + jnp.dot(kd_t, v, preferred_element_type=F32)
            y = inner + cross
            mu = jnp.mean(y, axis=-1, keepdims=True)
            yc = y - mu
            var = jnp.mean(yc * yc, axis=-1, keepdims=True)
            sl = slice(hh * RET_V_DIM, (hh + 1) * RET_V_DIM)
            yn = yc * lax.rsqrt(var + EPS) * gn_ref[:, sl]
            ya_ref[b, :, sl] = (gs_ref[b, :, sl].astype(F32) * yn).astype(BF16)

        for gg in range(SGU_GROUPS):
            sl = slice(gg * SGU_GROUP_DIM, (gg + 1) * SGU_GROUP_DIM)
            mixed = jnp.dot(ws[gg], vn_ref[b, :, sl], preferred_element_type=F32) + bs_ref[gg]
            ys_ref[b, :, sl] = (u_ref[b, :, sl].astype(F32) * mixed).astype(BF16)


def _chunk(proj, din, dq, dk, dc, gn, ws, bs, bsz, seq):
    n = proj.shape[0]
    nc = seq // CHUNK
    d = D_MODEL
    nb = math.gcd(bsz, CHUNK_BATCHES)
    proj3 = proj.reshape(bsz, seq, IN_WIDTH)

    def col(j):
        return pl.BlockSpec((nb, CHUNK, d), lambda b, c: (b, c, j))

    def const(shape):
        return pl.BlockSpec(shape, lambda b, c: (0,) * len(shape))

    ya, ys = pl.pallas_call(
        _chunk_kernel,
        grid=(bsz // nb, nc),
        in_specs=[col(0), col(1), col(2), col(3), col(4),
                  const(din.shape), const(dq.shape), const(dk.shape), const(dc.shape),
                  const(gn.shape), const(ws.shape), const(bs.shape)],
        out_specs=[col(0), col(0)],
        out_shape=[jax.ShapeDtypeStruct((bsz, seq, d), BF16), jax.ShapeDtypeStruct((bsz, seq, d), BF16)],
        scratch_shapes=[pltpu.VMEM((nb, RET_HEADS, RET_QK_DIM, RET_V_DIM), F32)],
        compiler_params=_cparams(("arbitrary", "arbitrary")),
        name="chunk",
    )(proj3, proj3, proj3, proj3, proj3, din, dq, dk, dc, gn, ws, bs)
    return ya.reshape(n, d), ys.reshape(n, d)


def _merge_kernel(ya_ref, ys_ref, sa_ref, sb_ref, x_ref, g1_ref, sh2_ref, sc2_ref, n2_ref,
                  wa_ref, wb_ref, wo_ref, x1_ref, h2t_ref):
    ba = jnp.dot(ya_ref[...], wa_ref[...], preferred_element_type=F32)
    bb = jnp.dot(ys_ref[...], wb_ref[...], preferred_element_type=F32)
    merged = sa_ref[...].astype(F32) * ba + sb_ref[...].astype(F32) * bb
    mo = jnp.dot(merged.astype(BF16), wo_ref[...], preferred_element_type=F32)
    x1 = x_ref[...] + g1_ref[0] * mo
    x1_ref[...] = x1
    h2t_ref[...] = (_rms(x1, n2_ref[...]) * (1.0 + sc2_ref[0]) + sh2_ref[0]).T.astype(BF16)


def _merge(ya, ys, proj, x2d, mod3, n2, wa, wb, wo, seq):
    n, d = x2d.shape
    tm = TM_PROJ
    tpb = seq // tm

    def modspec(k):
        return pl.BlockSpec((1, 1, d), lambda i: (i // tpb, 0, k))

    def wspec():
        return pl.BlockSpec((d, d), lambda i: (0, 0))

    return pl.pallas_call(
        _merge_kernel,
        grid=(n // tm,),
        in_specs=[pl.BlockSpec((tm, d), lambda i: (i, 0)),
                  pl.BlockSpec((tm, d), lambda i: (i, 0)),
                  pl.BlockSpec((tm, d), lambda i: (i, 5)),
                  pl.BlockSpec((tm, d), lambda i: (i, 6)),
                  pl.BlockSpec((tm, d), lambda i: (i, 0)),
                  modspec(2), modspec(3), modspec(4),
                  pl.BlockSpec((1, d), lambda i: (0, 0)),
                  wspec(), wspec(), wspec()],
        out_specs=[pl.BlockSpec((tm, d), lambda i: (i, 0)),
                   pl.BlockSpec((d, tm), lambda i: (0, i))],
        out_shape=[jax.ShapeDtypeStruct((n, d), F32), jax.ShapeDtypeStruct((d, n), BF16)],
        compiler_params=_cparams(("arbitrary",)),
        name="merge",
    )(ya, ys, proj, proj, x2d, mod3, mod3, mod3, n2, wa, wb, wo)


def _sort_network(n):
    pairs = []
    p = 1
    while p < n:
        k = p
        while k >= 1:
            for j in range(k % p, n - k, 2 * k):
                for i in range(min(k, n - j - k)):
                    if (i + j) // (2 * p) == (i + j + k) // (2 * p):
                        pairs.append((i + j, i + j + k))
            k //= 2
        p *= 2
    return pairs


def _sort_desc(vs):
    vs = list(vs)
    for i, j in _sort_network(PEER_TOPK):
        if j < len(vs):
            vs[i], vs[j] = jnp.maximum(vs[i], vs[j]), jnp.minimum(vs[i], vs[j])
    return vs


def _merge_top16_over_sublanes(vs):
    n = PEER_TOPK
    vs = list(vs) + [None] * (n - len(vs))
    for shift in (4, 2, 1):
        w = []
        for k in range(n):
            a, b = vs[k], vs[n - 1 - k]
            b = None if b is None else pltpu.roll(b, shift, 0)
            w.append(a if b is None else (b if a is None else jnp.maximum(a, b)))
        d = n // 2
        while d >= 1:
            for k in range(n):
                if k & d == 0:
                    w[k], w[k + d] = jnp.maximum(w[k], w[k + d]), jnp.minimum(w[k], w[k + d])
            d //= 2
        vs = w
    return vs


def _topk_kernel(h2t_ref, wq_ref, keys_ref, cnt_ref, e1_ref, rank_ref, e2_ref, qt_ref):
    nk = PEER_N_KEYS
    ntb = TOPK_TOKENS // LANES
    nv = nk // SUBLANES
    qt_ref[...] = jnp.dot(wq_ref[...], h2t_ref[...], preferred_element_type=F32).astype(BF16)

    def pack(rep, sub_iota):
        out = rep[SUBLANES - 1]
        for r in range(SUBLANES - 2, -1, -1):
            out = jnp.where(sub_iota == r, rep[r], out)
        return out

    def head_body(hh, carry):
        sub_iota = lax.broadcasted_iota(jnp.int32, (SUBLANES, LANES), 0)
        sts = []
        for p in range(2):
            off = pl.multiple_of((hh * 2 + p) * nk, nk)
            sts.append(jnp.dot(keys_ref[hh * 2 + p], qt_ref[pl.ds(off, nk), :],
                               preferred_element_type=F32))
        for tb in range(ntb):
            lsl = slice(tb * LANES, (tb + 1) * LANES)
            rows = [[sts[p][r * SUBLANES:(r + 1) * SUBLANES, lsl] for r in range(nv)] for p in range(2)]
            a1 = _merge_top16_over_sublanes(_sort_desc(rows[0]))
            a2 = _merge_top16_over_sublanes(_sort_desc(rows[1]))
            a1lo, a1hi = pack(a1[:SUBLANES], sub_iota), pack(a1[SUBLANES:], sub_iota)
            a2lo, a2hi = pack(a2[:SUBLANES], sub_iota), pack(a2[SUBLANES:], sub_iota)
            tail = sub_iota >= 2
            cands = [a1lo + a2[0], a1hi + a2[0], a1lo + a2[1],
                     jnp.where(tail, a1[0] + a2lo, NEG_INF), a1[0] + a2hi,
                     jnp.where(tail, a1[1] + a2lo, NEG_INF)]
            cands += [jnp.where(tail, a1lo + a2[l], NEG_INF) for l in (2, 3, 4)]
            top = _merge_top16_over_sublanes(_sort_desc(cands))
            tau = top[PEER_TOPK - 1]
            z = functools.reduce(lambda x, y: x + y, [jnp.exp(t - top[0]) for t in top])
            rz = 1.0 / z
            for rp in range(nv // 2):
                ranks, e2s = [], []
                for r in (2 * rp, 2 * rp + 1):
                    rsl = slice(r * SUBLANES, (r + 1) * SUBLANES)
                    s1v, s2v = rows[0][r], rows[1][r]
                    cnt = jnp.zeros((SUBLANES, LANES), F32)
                    rank = jnp.zeros((SUBLANES, LANES), F32)
                    for l in range(PEER_TOPK):
                        cnt = jnp.where(s1v + a2[l] >= tau, l + 1.0, cnt)
                        rank = jnp.where(a2[l] > s2v, l + 1.0, rank)
                    cnt_ref[tb, hh, rsl, :] = cnt
                    e1_ref[tb, hh, rsl, :] = jnp.exp(s1v - a1[0]) * rz
                    ranks.append(rank)
                    e2s.append(jnp.exp(s2v - a2[0]))
                psl = slice(rp * SUBLANES, (rp + 1) * SUBLANES)
                rank_ref[tb, hh, psl, :] = pltpu.bitcast(jnp.concatenate(ranks, axis=0).astype(BF16), jnp.uint32)
                e2_ref[tb, hh, psl, :] = pltpu.bitcast(jnp.concatenate(e2s, axis=0).astype(BF16), jnp.uint32)
        return carry

    lax.fori_loop(0, PEER_HEADS, head_body, 0)


def _topk(h2t, wq_t, keys):
    d, n = h2t.shape
    tt = TOPK_TOKENS
    ntb = tt // LANES
    nb = n // LANES
    shape = (nb, PEER_HEADS, PEER_N_KEYS, LANES)
    pshape = (nb, PEER_HEADS, PEER_N_KEYS // 2, LANES)
    bigspec = pl.BlockSpec((ntb,) + shape[1:], lambda i: (i, 0, 0, 0))
    pspec = pl.BlockSpec((ntb,) + pshape[1:], lambda i: (i, 0, 0, 0))
    return pl.pallas_call(
        _topk_kernel,
        grid=(n // tt,),
        in_specs=[pl.BlockSpec((d, tt), lambda i: (0, i)),
                  pl.BlockSpec(wq_t.shape, lambda i: (0, 0)),
                  pl.BlockSpec(keys.shape, lambda i: (0, 0, 0))],
        out_specs=[bigspec, bigspec, pspec, pspec],
        out_shape=[jax.ShapeDtypeStruct(shape, F32), jax.ShapeDtypeStruct(shape, F32),
                   jax.ShapeDtypeStruct(pshape, jnp.uint32), jax.ShapeDtypeStruct(pshape, jnp.uint32)],
        scratch_shapes=[pltpu.VMEM((wq_t.shape[0], tt), BF16)],
        compiler_params=_cparams(("arbitrary",)),
        name="topk",
    )(h2t, wq_t, keys)


def _peer_kernel(h2t_ref, u_ref, vt_ref, cnt_ref, e1_ref, rank_ref, e2_ref, x1_ref, g2_ref, fg_ref,
                 o_ref, acc_ref, st_ref, a_ref):
    e = pl.program_id(1)
    ntb = PEER_TOKENS // LANES
    nk = PEER_N_KEYS
    npass = PEER_GROUP // PEER_PASS
    ni = PEER_PASS // nk
    nrow = nk // PACKED_ROWS
    nq = PEER_EXPERTS // PEER_GROUP

    @pl.when(e == 0)
    def _():
        acc_ref[...] = jnp.zeros_like(acc_ref)

    def packed_row(ref, tb, hh, key):
        return jnp.broadcast_to(ref[tb, hh, pl.ds(key, 1), :], (PACKED_ROWS, LANES)).astype(BF16)

    def gates(g):
        def pass_body(ps, carry):
            key0 = (e * nq + g) * (PEER_GROUP // nk) + ps * ni
            rbase = pl.multiple_of(ps * PEER_PASS, PEER_PASS)
            for tb in range(ntb):
                lsl = slice(tb * LANES, (tb + 1) * LANES)
                gs = [[jnp.zeros((PACKED_ROWS, LANES), BF16) for _ in range(nrow)] for _ in range(ni)]
                for hh in range(PEER_HEADS):
                    cnts = [packed_row(cnt_ref, tb, hh, key0 + ii) for ii in range(ni)]
                    e1s = [packed_row(e1_ref, tb, hh, key0 + ii) for ii in range(ni)]
                    for r in range(nrow):
                        wsl = slice(r * SUBLANES, (r + 1) * SUBLANES)
                        rk = pltpu.bitcast(rank_ref[tb, hh, wsl, :], BF16)
                        e2v = pltpu.bitcast(e2_ref[tb, hh, wsl, :], BF16)
                        for ii in range(ni):
                            gs[ii][r] = gs[ii][r] + jnp.where(rk < cnts[ii], e2v, jnp.zeros_like(e2v)) * e1s[ii]
                for ii in range(ni):
                    for r in range(nrow):
                        rows = pl.ds(rbase + ii * nk + r * PACKED_ROWS, PACKED_ROWS)
                        a_ref[rows, lsl] = _gelu(st_ref[rows, lsl]).astype(BF16) * gs[ii][r]
            return carry

        lax.fori_loop(0, npass, pass_body, 0)

    for g in range(nq):
        gsl = slice(g * PEER_GROUP, (g + 1) * PEER_GROUP)
        st_ref[...] = jnp.dot(u_ref[gsl, :], h2t_ref[...], preferred_element_type=F32)
        gates(g)
        acc_ref[...] += jnp.dot(vt_ref[:, gsl], a_ref[...], preferred_element_type=F32)

    @pl.when(e == pl.num_programs(1) - 1)
    def _():
        x2 = x1_ref[...] + g2_ref[0] * acc_ref[...].T
        o_ref[...] = _rms(x2, fg_ref[...])


def _peer(h2t, u_bf, vt_bf, cnt, e1, rank, e2, x1, mod3, fg, seq):
    d, n = h2t.shape
    tt = PEER_TOKENS
    eb = PEER_EXPERTS
    gq = PEER_GROUP
    ntb = tt // LANES
    tpb = seq // tt
    n_exp = u_bf.shape[0]
    bigspec = pl.BlockSpec((ntb, PEER_HEADS, PEER_N_KEYS, LANES), lambda t, e: (t, 0, 0, 0))
    pspec = pl.BlockSpec((ntb, PEER_HEADS, PEER_N_KEYS // 2, LANES), lambda t, e: (t, 0, 0, 0))
    return pl.pallas_call(
        _peer_kernel,
        grid=(n // tt, n_exp // eb),
        in_specs=[pl.BlockSpec((d, tt), lambda t, e: (0, t)),
                  pl.BlockSpec((eb, d), lambda t, e: (e, 0)),
                  pl.BlockSpec((d, eb), lambda t, e: (0, e)),
                  bigspec, bigspec, pspec, pspec,
                  pl.BlockSpec((tt, d), lambda t, e: (t, 0)),
                  pl.BlockSpec((1, 1, d), lambda t, e: (t // tpb, 0, 5)),
                  pl.BlockSpec((1, d), lambda t, e: (0, 0))],
        out_specs=pl.BlockSpec((tt, d), lambda t, e: (t, 0)),
        out_shape=jax.ShapeDtypeStruct((n, d), F32),
        scratch_shapes=[pltpu.VMEM((d, tt), F32),
                        pltpu.VMEM((gq, tt), F32),
                        pltpu.VMEM((gq, tt), BF16)],
        compiler_params=_cparams(("arbitrary", "arbitrary")),
        name="peer",
    )(h2t, u_bf, vt_bf, cnt, e1, rank, e2, x1, mod3, fg)


def _retention_tables():
    hcount = RET_HEADS
    c = CHUNK
    gamma = 1.0 - 2.0 ** (-5.0 - jnp.arange(hcount, dtype=F32))
    log_g = jnp.log(gamma)
    idx = jnp.arange(c, dtype=F32)
    diff = idx[:, None] - idx[None, :]
    din = jnp.where((diff >= 0)[None], jnp.exp(log_g[:, None, None] * jnp.maximum(diff, 0.0)[None]), 0.0)
    dq = jnp.exp(log_g[:, None] * (idx[None, :] + 1.0))
    dk = jnp.exp(log_g[:, None] * (c - 1.0 - idx[None, :]))
    dc = jnp.exp(log_g * c)
    dq = jnp.broadcast_to(dq[:, :, None], (hcount, c, RET_V_DIM))
    dk = jnp.broadcast_to(dk[:, :, None], (hcount, c, RET_QK_DIM))
    dc = jnp.broadcast_to(dc[:, None, None], (hcount, RET_QK_DIM, RET_V_DIM))
    return din.astype(F32), dq.astype(F32), dk.astype(F32), dc.astype(F32)


def _rotary_tables(seq):
    half = RET_QK_DIM // 2
    pos = jnp.arange(seq, dtype=F32)
    inv = ROPE_BASE ** (-jnp.arange(half, dtype=F32) * 2.0 / RET_QK_DIM)
    ang = pos[:, None] * inv[None, :]
    cos = jnp.cos(ang)
    sin = jnp.sin(ang)
    return jnp.concatenate([cos, cos], axis=-1), jnp.concatenate([-sin, sin], axis=-1)


def kernel(x, c, w_ada, b_ada, norm1_g, w_in, ret_gn_g, sgu_ln_g, sgu_ln_b, sgu_w, sgu_b, w_ret_out,
           w_sgu_out, w_out, norm2_g, peer_w_q, peer_sub_keys, peer_u, peer_v, final_g):
    bsz, seq, d = x.shape
    n = bsz * seq
    depth = w_ada.shape[0]
    assert d == D_MODEL and seq % TM_PROJ == 0 and seq % PEER_TOKENS == 0 and seq % TOPK_TOKENS == 0
    cos_t, sin_t = _rotary_tables(seq)
    din, dq, dk, dc = _retention_tables()
    xc = x.reshape(n, d)
    for l in range(depth):
        mod3 = _ada(c, w_ada[l], b_ada[l]).reshape(bsz, 1, N_MOD * d)
        proj = _inproj(xc, mod3, norm1_g[l].reshape(1, d), cos_t, sin_t,
                       sgu_ln_g[l].reshape(1, d), sgu_ln_b[l].reshape(1, d), w_in[l].astype(BF16), seq)
        bs = jnp.broadcast_to(sgu_b[l][:, :, None], (SGU_GROUPS, CHUNK, SGU_GROUP_DIM))
        ya, ys = _chunk(proj, din, dq, dk, dc, ret_gn_g[l].reshape(1, d), sgu_w[l], bs, bsz, seq)
        x1, h2t = _merge(ya, ys, proj, xc, mod3, norm2_g[l].reshape(1, d), w_ret_out[l].astype(BF16),
                        w_sgu_out[l].astype(BF16), w_out[l].astype(BF16), seq)
        wq_t = peer_w_q[l].T.astype(BF16)
        keys = peer_sub_keys[l].reshape(PEER_HEADS * 2, PEER_N_KEYS, -1).astype(BF16)
        cnt, e1, rank, e2 = _topk(h2t, wq_t, keys)
        assert depth == 1
        xc = _peer(h2t, peer_u[l].astype(BF16), peer_v[l].T.astype(BF16), cnt, e1, rank, e2, x1, mod3,
                   final_g.reshape(1, d), seq)
    return xc.reshape(bsz, seq, d)
```

```python
import functools
import math

import jax
import jax.numpy as jnp
from jax import lax
from jax.experimental import pallas as pl
from jax.experimental.pallas import tpu as pltpu

F32 = jnp.float32
BF16 = jnp.bfloat16

D_MODEL = 1024
RET_HEADS = 4
RET_QK_DIM = 128
RET_V_DIM = 256
CHUNK = 128
SGU_GROUPS = 4
SGU_GROUP_DIM = 256
IN_WIDTH = 7168
PEER_HEADS = 8
PEER_N_KEYS = 128
PEER_TOPK = 16
N_MOD = 6
EPS = 1e-6
ROPE_BASE = 10000.0
NEG_INF = float("-inf")

VMEM_LIMIT_BYTES = 56 * 1024 * 1024

TM_PROJ = 512
CHUNK_BATCHES = 4
TOPK_TOKENS = 512
PEER_TOKENS = 512
PEER_EXPERTS = 1024
PEER_GROUP = 1024
PEER_PASS = 256
LANES = 128
SUBLANES = 8
PACKED_ROWS = 16


def _gelu(x):
    return 0.5 * x * (1.0 + lax.erf(x * (1.0 / math.sqrt(2.0))))


def _rms(x, g):
    ms = jnp.mean(x * x, axis=-1, keepdims=True)
    return x * lax.rsqrt(ms + EPS) * g


def _cparams(sem):
    return pltpu.CompilerParams(dimension_semantics=sem, vmem_limit_bytes=VMEM_LIMIT_BYTES)


def _ada_kernel(c_ref, w_ref, b_ref, o_ref):
    c = c_ref[...]
    ca = c * jax.nn.sigmoid(c)
    o_ref[...] = jnp.dot(ca.astype(BF16), w_ref[...].astype(BF16),
                         preferred_element_type=F32) + b_ref[...]


def _ada(c, w, b):
    bsz, d = c.shape
    n_out = w.shape[1]
    return pl.pallas_call(
        _ada_kernel,
        grid=(n_out // d,),
        in_specs=[pl.BlockSpec((bsz, d), lambda j: (0, 0)),
                  pl.BlockSpec((d, d), lambda j: (0, j)),
                  pl.BlockSpec((1, d), lambda j: (0, j))],
        out_specs=pl.BlockSpec((bsz, d), lambda j: (0, j)),
        out_shape=jax.ShapeDtypeStruct((bsz, n_out), F32),
        compiler_params=_cparams(("arbitrary",)),
        name="ada",
    )(c, w, b.reshape(1, n_out))


def _inproj_kernel(x_ref, sh_ref, sc_ref, g_ref, cos_ref, sin_ref, lng_ref, lnb_ref, w_ref, o_ref):
    h = (_rms(x_ref[...], g_ref[...]) * (1.0 + sc_ref[0]) + sh_ref[0]).astype(BF16)
    d = D_MODEL
    for j in range(IN_WIDTH // d):
        acc = jnp.dot(h, w_ref[:, j * d:(j + 1) * d], preferred_element_type=F32)
        if j == 0:
            cos = cos_ref[...]
            sin = sin_ref[...]
            for hh in range(2 * RET_HEADS):
                blk = acc[:, hh * RET_QK_DIM:(hh + 1) * RET_QK_DIM]
                r = blk * cos + pltpu.roll(blk, RET_QK_DIM // 2, 1) * sin
                if hh >= RET_HEADS:
                    r = r * (RET_QK_DIM ** -0.5)
                o_ref[:, hh * RET_QK_DIM:(hh + 1) * RET_QK_DIM] = r.astype(BF16)
            continue
        if j == 1:
            res = acc
        elif j == 2:
            res = acc * jax.nn.sigmoid(acc)
        elif j == 3:
            res = _gelu(acc)
        elif j == 4:
            a = _gelu(acc)
            mu = jnp.mean(a, axis=-1, keepdims=True)
            ac = a - mu
            var = jnp.mean(ac * ac, axis=-1, keepdims=True)
            res = ac * lax.rsqrt(var + EPS) * lng_ref[...] + lnb_ref[...]
        else:
            res = jax.nn.sigmoid(acc)
        o_ref[:, j * d:(j + 1) * d] = res.astype(BF16)


def _inproj(x2d, mod3, g, cos_t, sin_t, ln_g, ln_b, w_bf, seq):
    n, d = x2d.shape
    tm = TM_PROJ
    tpb = seq // tm
    return pl.pallas_call(
        _inproj_kernel,
        grid=(n // tm,),
        in_specs=[pl.BlockSpec((tm, d), lambda i: (i, 0)),
                  pl.BlockSpec((1, 1, d), lambda i: (i // tpb, 0, 0)),
                  pl.BlockSpec((1, 1, d), lambda i: (i // tpb, 0, 1)),
                  pl.BlockSpec((1, d), lambda i: (0, 0)),
                  pl.BlockSpec((tm, RET_QK_DIM), lambda i: (i % tpb, 0)),
                  pl.BlockSpec((tm, RET_QK_DIM), lambda i: (i % tpb, 0)),
                  pl.BlockSpec((1, d), lambda i: (0, 0)),
                  pl.BlockSpec((1, d), lambda i: (0, 0)),
                  pl.BlockSpec((d, IN_WIDTH), lambda i: (0, 0), pipeline_mode=pl.Buffered(1))],
        out_specs=pl.BlockSpec((tm, IN_WIDTH), lambda i: (i, 0)),
        out_shape=jax.ShapeDtypeStruct((n, IN_WIDTH), BF16),
        compiler_params=_cparams(("arbitrary",)),
        name="inproj",
    )(x2d, mod3, mod3, g, cos_t, sin_t, ln_g, ln_b, w_bf)


def _chunk_kernel(qk_ref, v_ref, gs_ref, u_ref, vn_ref, din_ref, dq_ref, dk_ref, dc_ref,
                  gn_ref, ws_ref, bs_ref, ya_ref, ys_ref, state_ref):
    @pl.when(pl.program_id(1) == 0)
    def _():
        state_ref[...] = jnp.zeros_like(state_ref)

    row = lax.broadcasted_iota(jnp.int32, (CHUNK, CHUNK), 0)
    col = lax.broadcasted_iota(jnp.int32, (CHUNK, CHUNK), 1)
    ws = [jnp.where(row >= col, ws_ref[gg], 0.0).astype(BF16) for gg in range(SGU_GROUPS)]

    for b in range(qk_ref.shape[0]):
        for hh in range(RET_HEADS):
            q = qk_ref[b, :, hh * RET_QK_DIM:(hh + 1) * RET_QK_DIM]
            k = qk_ref[b, :, (RET_HEADS + hh) * RET_QK_DIM:(RET_HEADS + hh + 1) * RET_QK_DIM]
            v = v_ref[b, :, hh * RET_V_DIM:(hh + 1) * RET_V_DIM]
            state = state_ref[b, hh]
            s = lax.dot_general(q, k, (((1,), (1,)), ((), ())), preferred_element_type=F32) * din_ref[hh]
            inner = jnp.dot(s.astype(BF16), v, preferred_element_type=F32)
            cross = jnp.dot(q, state.astype(BF16), preferred_element_type=F32) * dq_ref[hh]
            kd_t = (k.astype(F32) * dk_ref[hh]).T.astype(BF16)
            state_ref[b, hh] = state * dc_ref[hh] + jnp.dot(kd_t, v, preferred_element_type=F32)
            y = inner + cross
            mu = jnp.mean(y, axis=-1, keepdims=True)
            yc = y - mu
            var = jnp.mean(yc * yc, axis=-1, keepdims=True)
            sl = slice(hh * RET_V_DIM, (hh + 1) * RET_V_DIM)
            yn = yc * lax.rsqrt(var + EPS) * gn_ref[:, sl]
            ya_ref[b, :, sl] = (gs_ref[b, :, sl].astype(F32) * yn).astype(BF16)

        for gg in range(SGU_GROUPS):
            sl = slice(gg * SGU_GROUP_DIM, (gg + 1) * SGU_GROUP_DIM)
            mixed = jnp.dot(ws[gg], vn_ref[b, :, sl], preferred_element_type=F32) + bs_ref[gg]
            ys_ref[b, :, sl] = (u_ref[b, :, sl].astype(F32) * mixed).astype(BF16)


def _chunk(proj, din, dq, dk, dc, gn, ws, bs, bsz, seq):
    n = proj.shape[0]
    nc = seq // CHUNK
    d = D_MODEL
    nb = math.gcd(bsz, CHUNK_BATCHES)
    proj3 = proj.reshape(bsz, seq, IN_WIDTH)

    def col(j):
        return pl.BlockSpec((nb, CHUNK, d), lambda b, c: (b, c, j))

    def const(shape):
        return pl.BlockSpec(shape, lambda b, c: (0,) * len(shape))

    ya, ys = pl.pallas_call(
        _chunk_kernel,
        grid=(bsz // nb, nc),
        in_specs=[col(0), col(1), col(2), col(3), col(4),
                  const(din.shape), const(dq.shape), const(dk.shape), const(dc.shape),
                  const(gn.shape), const(ws.shape), const(bs.shape)],
        out_specs=[col(0), col(0)],
        out_shape=[jax.ShapeDtypeStruct((bsz, seq, d), BF16), jax.ShapeDtypeStruct((bsz, seq, d), BF16)],
        scratch_shapes=[pltpu.VMEM((nb, RET_HEADS, RET_QK_DIM, RET_V_DIM), F32)],
        compiler_params=_cparams(("arbitrary", "arbitrary")),
        name="chunk",
    )(proj3, proj3, proj3, proj3, proj3, din, dq, dk, dc, gn, ws, bs)
    return ya.reshape(n, d), ys.reshape(n, d)


def _merge_kernel(ya_ref, ys_ref, sa_ref, sb_ref, x_ref, g1_ref, sh2_ref, sc2_ref, n2_ref,
                  wa_ref, wb_ref, wo_ref, x1_ref, h2t_ref):
    ba = jnp.dot(ya_ref[...], wa_ref[...], preferred_element_type=F32)
    bb = jnp.dot(ys_ref[...], wb_ref[...], preferred_element_type=F32)
    merged = sa_ref[...].astype(F32) * ba + sb_ref[...].astype(F32) * bb
    mo = jnp.dot(merged.astype(BF16), wo_ref[...], preferred_element_type=F32)
    x1 = x_ref[...] + g1_ref[0] * mo
    x1_ref[...] = x1
    h2t_ref[...] = (_rms(x1, n2_ref[...]) * (1.0 + sc2_ref[0]) + sh2_ref[0]).T.astype(BF16)


def _merge(ya, ys, proj, x2d, mod3, n2, wa, wb, wo, seq):
    n, d = x2d.shape
    tm = TM_PROJ
    tpb = seq // tm

    def modspec(k):
        return pl.BlockSpec((1, 1, d), lambda i: (i // tpb, 0, k))

    def wspec():
        return pl.BlockSpec((d, d), lambda i: (0, 0))

    return pl.pallas_call(
        _merge_kernel,
        grid=(n // tm,),
        in_specs=[pl.BlockSpec((tm, d), lambda i: (i, 0)),
                  pl.BlockSpec((tm, d), lambda i: (i, 0)),
                  pl.BlockSpec((tm, d), lambda i: (i, 5)),
                  pl.BlockSpec((tm, d), lambda i: (i, 6)),
                  pl.BlockSpec((tm, d), lambda i: (i, 0)),
                  modspec(2), modspec(3), modspec(4),
                  pl.BlockSpec((1, d), lambda i: (0, 0)),
                  wspec(), wspec(), wspec()],
        out_specs=[pl.BlockSpec((tm, d), lambda i: (i, 0)),
                   pl.BlockSpec((d, tm), lambda i: (0, i))],
        out_shape=[jax.ShapeDtypeStruct((n, d), F32), jax.ShapeDtypeStruct((d, n), BF16)],
        compiler_params=_cparams(("arbitrary",)),
        name="merge",
    )(ya, ys, proj, proj, x2d, mod3, mod3, mod3, n2, wa, wb, wo)


def _sort_network(n):
    pairs = []
    p = 1
    while p < n:
        k = p
        while k >= 1:
            for j in range(k % p, n - k, 2 * k):
                for i in range(min(k, n - j - k)):
                    if (i + j) // (2 * p) == (i + j + k) // (2 * p):
                        pairs.append((i + j, i + j + k))
            k //= 2
        p *= 2
    return pairs


def _sort_desc(vs):
    vs = list(vs)
    for i, j in _sort_network(PEER_TOPK):
        if j < len(vs):
            vs[i], vs[j] = jnp.maximum(vs[i], vs[j]), jnp.minimum(vs[i], vs[j])
    return vs


def _merge_top16_over_sublanes(vs):
    n = PEER_TOPK
    vs = list(vs) + [None] * (n - len(vs))
    for shift in (4, 2, 1):
        w = []
        for k in range(n):
            a, b = vs[k], vs[n - 1 - k]
            b = None if b is None else pltpu.roll(b, shift, 0)
            w.append(a if b is None else (b if a is None else jnp.maximum(a, b)))
        d = n // 2
        while d >= 1:
            for k in range(n):
                if k & d == 0:
                    w[k], w[k + d] = jnp.maximum(w[k], w[k + d]), jnp.minimum(w[k], w[k + d])
            d //= 2
        vs = w
    return vs


def _prefix_count(a, pred):
    assert len(a) == PEER_TOPK == 16
    m1 = pred(a[7])
    m2 = pred(jnp.where(m1, a[11], a[3]))
    m3 = pred(jnp.where(m1, jnp.where(m2, a[13], a[9]), jnp.where(m2, a[5], a[1])))
    hi = jnp.where(m2, jnp.where(m3, a[14], a[12]), jnp.where(m3, a[10], a[8]))
    lo = jnp.where(m2, jnp.where(m3, a[6], a[4]), jnp.where(m3, a[2], a[0]))
    m4 = pred(jnp.where(m1, hi, lo))
    c = (jnp.where(m1, 8.0, 0.0) + jnp.where(m2, 4.0, 0.0)) + (jnp.where(m3, 2.0, 0.0) + jnp.where(m4, 1.0, 0.0))
    return jnp.where(pred(a[15]), 16.0, c)


def _topk_kernel(h2t_ref, wq_ref, keys_ref, cnt_ref, e1_ref, rank_ref, e2_ref, qt_ref):
    nk = PEER_N_KEYS
    ntb = TOPK_TOKENS // LANES
    nv = nk // SUBLANES
    qt_ref[...] = jnp.dot(wq_ref[...], h2t_ref[...], preferred_element_type=F32).astype(BF16)

    def pack(rep, sub_iota):
        out = rep[SUBLANES - 1]
        for r in range(SUBLANES - 2, -1, -1):
            out = jnp.where(sub_iota == r, rep[r], out)
        return out

    def head_body(hh, carry):
        sub_iota = lax.broadcasted_iota(jnp.int32, (SUBLANES, LANES), 0)
        sts = []
        for p in range(2):
            off = pl.multiple_of((hh * 2 + p) * nk, nk)
            sts.append(jnp.dot(keys_ref[hh * 2 + p], qt_ref[pl.ds(off, nk), :],
                               preferred_element_type=F32))
        for tb in range(ntb):
            lsl = slice(tb * LANES, (tb + 1) * LANES)
            rows = [[sts[p][r * SUBLANES:(r + 1) * SUBLANES, lsl] for r in range(nv)] for p in range(2)]
            a1 = _merge_top16_over_sublanes(_sort_desc(rows[0]))
            a2 = _merge_top16_over_sublanes(_sort_desc(rows[1]))
            a1lo, a1hi = pack(a1[:SUBLANES], sub_iota), pack(a1[SUBLANES:], sub_iota)
            a2lo, a2hi = pack(a2[:SUBLANES], sub_iota), pack(a2[SUBLANES:], sub_iota)
            tail = sub_iota >= 2
            cands = [a1lo + a2[0], a1hi + a2[0], a1lo + a2[1],
                     jnp.where(tail, a1[0] + a2lo, NEG_INF), a1[0] + a2hi,
                     jnp.where(tail, a1[1] + a2lo, NEG_INF)]
            cands += [jnp.where(tail, a1lo + a2[l], NEG_INF) for l in (2, 3, 4)]
            top = _merge_top16_over_sublanes(_sort_desc(cands))
            tau = top[PEER_TOPK - 1]
            z = functools.reduce(lambda x, y: x + y, [jnp.exp(t - top[0]) for t in top])
            rz = 1.0 / z
            for rp in range(nv // 2):
                ranks, e2s = [], []
                for r in (2 * rp, 2 * rp + 1):
                    rsl = slice(r * SUBLANES, (r + 1) * SUBLANES)
                    s1v, s2v = rows[0][r], rows[1][r]
                    cnt = _prefix_count(a2, lambda t: s1v + t >= tau)
                    rank = _prefix_count(a2, lambda t: t > s2v)
                    cnt_ref[tb, hh, rsl, :] = cnt
                    e1_ref[tb, hh, rsl, :] = jnp.exp(s1v - a1[0]) * rz
                    ranks.append(rank)
                    e2s.append(jnp.exp(s2v - a2[0]))
                psl = slice(rp * SUBLANES, (rp + 1) * SUBLANES)
                rank_ref[tb, hh, psl, :] = pltpu.bitcast(jnp.concatenate(ranks, axis=0).astype(BF16), jnp.uint32)
                e2_ref[tb, hh, psl, :] = pltpu.bitcast(jnp.concatenate(e2s, axis=0).astype(BF16), jnp.uint32)
        return carry

    lax.fori_loop(0, PEER_HEADS, head_body, 0)


def _topk(h2t, wq_t, keys):
    d, n = h2t.shape
    tt = TOPK_TOKENS
    ntb = tt // LANES
    nb = n // LANES
    shape = (nb, PEER_HEADS, PEER_N_KEYS, LANES)
    pshape = (nb, PEER_HEADS, PEER_N_KEYS // 2, LANES)
    bigspec = pl.BlockSpec((ntb,) + shape[1:], lambda i: (i, 0, 0, 0))
    pspec = pl.BlockSpec((ntb,) + pshape[1:], lambda i: (i, 0, 0, 0))
    return pl.pallas_call(
        _topk_kernel,
        grid=(n // tt,),
        in_specs=[pl.BlockSpec((d, tt), lambda i: (0, i)),
                  pl.BlockSpec(wq_t.shape, lambda i: (0, 0)),
                  pl.BlockSpec(keys.shape, lambda i: (0, 0, 0))],
        out_specs=[bigspec, bigspec, pspec, pspec],
        out_shape=[jax.ShapeDtypeStruct(shape, F32), jax.ShapeDtypeStruct(shape, F32),
                   jax.ShapeDtypeStruct(pshape, jnp.uint32), jax.ShapeDtypeStruct(pshape, jnp.uint32)],
        scratch_shapes=[pltpu.VMEM((wq_t.shape[0], tt), BF16)],
        compiler_params=_cparams(("arbitrary",)),
        name="topk",
    )(h2t, wq_t, keys)


def _peer_kernel(h2t_ref, u_ref, vt_ref, cnt_ref, e1_ref, rank_ref, e2_ref, x1_ref, g2_ref, fg_ref,
                 o_ref, acc_ref, st_ref, a_ref):
    e = pl.program_id(1)
    ntb = PEER_TOKENS // LANES
    nk = PEER_N_KEYS
    npass = PEER_GROUP // PEER_PASS
    ni = PEER_PASS // nk
    nrow = nk // PACKED_ROWS
    nq = PEER_EXPERTS // PEER_GROUP

    @pl.when(e == 0)
    def _():
        acc_ref[...] = jnp.zeros_like(acc_ref)

    def packed_row(ref, tb, hh, key):
        return jnp.broadcast_to(ref[tb, hh, key:key + 1, :], (PACKED_ROWS, LANES)).astype(BF16)

    def gate_pass(g, ps):
        key0 = g * (PEER_GROUP // nk) + ps * ni
        rbase = ps * PEER_PASS
        for tb in range(ntb):
            lsl = slice(tb * LANES, (tb + 1) * LANES)
            gs = [[jnp.zeros((PACKED_ROWS, LANES), BF16) for _ in range(nrow)] for _ in range(ni)]
            for hh in range(PEER_HEADS):
                cnts = [packed_row(cnt_ref, tb, hh, key0 + ii) for ii in range(ni)]
                e1s = [packed_row(e1_ref, tb, hh, key0 + ii) for ii in range(ni)]
                for r in range(nrow):
                    wsl = slice(r * SUBLANES, (r + 1) * SUBLANES)
                    rk = pltpu.bitcast(rank_ref[tb, hh, wsl, :], BF16)
                    e2v = pltpu.bitcast(e2_ref[tb, hh, wsl, :], BF16)
                    for ii in range(ni):
                        gs[ii][r] = gs[ii][r] + jnp.where(rk < cnts[ii], e2v, jnp.zeros_like(e2v)) * e1s[ii]
            for ii in range(ni):
                for r in range(nrow):
                    r0 = rbase + ii * nk + r * PACKED_ROWS
                    rows = slice(r0, r0 + PACKED_ROWS)
                    a_ref[rows, lsl] = _gelu(st_ref[rows, lsl].astype(BF16)) * gs[ii][r]

    def gates(g):
        for ps in range(npass):
            pl.when(e >= 0)(functools.partial(gate_pass, g, ps))

    for g in range(nq):
        gsl = slice(g * PEER_GROUP, (g + 1) * PEER_GROUP)
        st_ref[...] = jnp.dot(u_ref[gsl, :], h2t_ref[...], preferred_element_type=F32)
        gates(g)
        acc_ref[...] += jnp.dot(vt_ref[:, gsl], a_ref[...], preferred_element_type=F32)

    @pl.when(e == pl.num_programs(1) - 1)
    def _():
        x2 = x1_ref[...] + g2_ref[0] * acc_ref[...].T
        o_ref[...] = _rms(x2, fg_ref[...])


def _peer(h2t, u_bf, vt_bf, cnt, e1, rank, e2, x1, mod3, fg, seq):
    d, n = h2t.shape
    tt = PEER_TOKENS
    eb = PEER_EXPERTS
    gq = PEER_GROUP
    ntb = tt // LANES
    tpb = seq // tt
    n_exp = u_bf.shape[0]
    rowspec = pl.BlockSpec((ntb, PEER_HEADS, eb // PEER_N_KEYS, LANES), lambda t, e: (t, 0, e, 0))
    pspec = pl.BlockSpec((ntb, PEER_HEADS, PEER_N_KEYS // 2, LANES), lambda t, e: (t, 0, 0, 0))
    assert eb // PEER_N_KEYS == SUBLANES
    return pl.pallas_call(
        _peer_kernel,
        grid=(n // tt, n_exp // eb),
        in_specs=[pl.BlockSpec((d, tt), lambda t, e: (0, t)),
                  pl.BlockSpec((eb, d), lambda t, e: (e, 0)),
                  pl.BlockSpec((d, eb), lambda t, e: (0, e)),
                  rowspec, rowspec, pspec, pspec,
                  pl.BlockSpec((tt, d), lambda t, e: (t, 0)),
                  pl.BlockSpec((1, 1, d), lambda t, e: (t // tpb, 0, 5)),
                  pl.BlockSpec((1, d), lambda t, e: (0, 0))],
        out_specs=pl.BlockSpec((tt, d), lambda t, e: (t, 0)),
        out_shape=jax.ShapeDtypeStruct((n, d), F32),
        scratch_shapes=[pltpu.VMEM((d, tt), F32),
                        pltpu.VMEM((gq, tt), F32),
                        pltpu.VMEM((gq, tt), BF16)],
        compiler_params=_cparams(("arbitrary", "arbitrary")),
        name="peer",
    )(h2t, u_bf, vt_bf, cnt, e1, rank, e2, x1, mod3, fg)


def _retention_tables():
    hcount = RET_HEADS
    c = CHUNK
    gamma = 1.0 - 2.0 ** (-5.0 - jnp.arange(hcount, dtype=F32))
    log_g = jnp.log(gamma)
    idx = jnp.arange(c, dtype=F32)
    diff = idx[:, None] - idx[None, :]
    din = jnp.where((diff >= 0)[None], jnp.exp(log_g[:, None, None] * jnp.maximum(diff, 0.0)[None]), 0.0)
    dq = jnp.exp(log_g[:, None] * (idx[None, :] + 1.0))
    dk = jnp.exp(log_g[:, None] * (c - 1.0 - idx[None, :]))
    dc = jnp.exp(log_g * c)
    dq = jnp.broadcast_to(dq[:, :, None], (hcount, c, RET_V_DIM))
    dk = jnp.broadcast_to(dk[:, :, None], (hcount, c, RET_QK_DIM))
    dc = jnp.broadcast_to(dc[:, None, None], (hcount, RET_QK_DIM, RET_V_DIM))
    return din.astype(F32), dq.astype(F32), dk.astype(F32), dc.astype(F32)


def _rotary_tables(seq):
    half = RET_QK_DIM // 2
    pos = jnp.arange(seq, dtype=F32)
    inv = ROPE_BASE ** (-jnp.arange(half, dtype=F32) * 2.0 / RET_QK_DIM)
    ang = pos[:, None] * inv[None, :]
    cos = jnp.cos(ang)
    sin = jnp.sin(ang)
    return jnp.concatenate([cos, cos], axis=-1), jnp.concatenate([-sin, sin], axis=-1)


def kernel(x, c, w_ada, b_ada, norm1_g, w_in, ret_gn_g, sgu_ln_g, sgu_ln_b, sgu_w, sgu_b, w_ret_out,
           w_sgu_out, w_out, norm2_g, peer_w_q, peer_sub_keys, peer_u, peer_v, final_g):
    bsz, seq, d = x.shape
    n = bsz * seq
    depth = w_ada.shape[0]
    assert d == D_MODEL and seq % TM_PROJ == 0 and seq % PEER_TOKENS == 0 and seq % TOPK_TOKENS == 0
    cos_t, sin_t = _rotary_tables(seq)
    din, dq, dk, dc = _retention_tables()
    xc = x.reshape(n, d)
    for l in range(depth):
        mod3 = _ada(c, w_ada[l], b_ada[l]).reshape(bsz, 1, N_MOD * d)
        proj = _inproj(xc, mod3, norm1_g[l].reshape(1, d), cos_t, sin_t,
                       sgu_ln_g[l].reshape(1, d), sgu_ln_b[l].reshape(1, d), w_in[l].astype(BF16), seq)
        bs = jnp.broadcast_to(sgu_b[l][:, :, None], (SGU_GROUPS, CHUNK, SGU_GROUP_DIM))
        ya, ys = _chunk(proj, din, dq, dk, dc, ret_gn_g[l].reshape(1, d), sgu_w[l], bs, bsz, seq)
        x1, h2t = _merge(ya, ys, proj, xc, mod3, norm2_g[l].reshape(1, d), w_ret_out[l].astype(BF16),
                        w_sgu_out[l].astype(BF16), w_out[l].astype(BF16), seq)
        wq_t = peer_w_q[l].T.astype(BF16)
        keys = peer_sub_keys[l].reshape(PEER_HEADS * 2, PEER_N_KEYS, -1).astype(BF16)
        cnt, e1, rank, e2 = _topk(h2t, wq_t, keys)
        assert depth == 1
        xc = _peer(h2t, peer_u[l].astype(BF16), peer_v[l].T.astype(BF16), cnt, e1, rank, e2, x1, mod3,
                   final_g.reshape(1, d), seq)
    return xc.reshape(bsz, seq, d)
```

```python
import functools
import math

import jax
import jax.numpy as jnp
from jax import lax
from jax.experimental import pallas as pl
from jax.experimental.pallas import tpu as pltpu

F32 = jnp.float32
BF16 = jnp.bfloat16

D_MODEL = 1024
RET_HEADS = 4
RET_QK_DIM = 128
RET_V_DIM = 256
CHUNK = 128
SGU_GROUPS = 4
SGU_GROUP_DIM = 256
IN_WIDTH = 7168
PEER_HEADS = 8
PEER_N_KEYS = 128
PEER_TOPK = 16
N_MOD = 6
EPS = 1e-6
ROPE_BASE = 10000.0
NEG_INF = float("-inf")

VMEM_LIMIT_BYTES = 56 * 1024 * 1024

TM_PROJ = 512
CHUNK_BATCHES = 8
TOPK_TOKENS = 512
PEER_TOKENS = 1024
PEER_EXPERTS = 1024
PEER_GROUP = 1024
PEER_PASS = 256
LANES = 128
SUBLANES = 8
PACKED_ROWS = 16


def _gelu(x):
    return 0.5 * x * (1.0 + lax.erf(x * (1.0 / math.sqrt(2.0))))


def _rms(x, g):
    ms = jnp.mean(x * x, axis=-1, keepdims=True)
    return x * lax.rsqrt(ms + EPS) * g


def _cparams(sem):
    return pltpu.CompilerParams(dimension_semantics=sem, vmem_limit_bytes=VMEM_LIMIT_BYTES)


def _ada_kernel(c_ref, w_ref, b_ref, o_ref):
    c = c_ref[...]
    ca = c * jax.nn.sigmoid(c)
    o_ref[...] = jnp.dot(ca.astype(BF16), w_ref[...].astype(BF16),
                         preferred_element_type=F32) + b_ref[...]


def _ada(c, w, b):
    bsz, d = c.shape
    n_out = w.shape[1]
    return pl.pallas_call(
        _ada_kernel,
        grid=(n_out // d,),
        in_specs=[pl.BlockSpec((bsz, d), lambda j: (0, 0)),
                  pl.BlockSpec((d, d), lambda j: (0, j)),
                  pl.BlockSpec((1, d), lambda j: (0, j))],
        out_specs=pl.BlockSpec((bsz, d), lambda j: (0, j)),
        out_shape=jax.ShapeDtypeStruct((bsz, n_out), F32),
        compiler_params=_cparams(("arbitrary",)),
        name="ada",
    )(c, w, b.reshape(1, n_out))


def _inproj_kernel(x_ref, sh_ref, sc_ref, g_ref, cos_ref, sin_ref, lng_ref, lnb_ref, w_ref, o_ref):
    h = (_rms(x_ref[...], g_ref[...]) * (1.0 + sc_ref[0]) + sh_ref[0]).astype(BF16)
    d = D_MODEL
    for j in range(IN_WIDTH // d):
        acc = jnp.dot(h, w_ref[:, j * d:(j + 1) * d], preferred_element_type=F32)
        if j == 0:
            cos = cos_ref[...]
            sin = sin_ref[...]
            for hh in range(2 * RET_HEADS):
                blk = acc[:, hh * RET_QK_DIM:(hh + 1) * RET_QK_DIM]
                r = blk * cos + pltpu.roll(blk, RET_QK_DIM // 2, 1) * sin
                if hh >= RET_HEADS:
                    r = r * (RET_QK_DIM ** -0.5)
                o_ref[:, hh * RET_QK_DIM:(hh + 1) * RET_QK_DIM] = r.astype(BF16)
            continue
        if j == 1:
            res = acc
        elif j == 2:
            res = acc * jax.nn.sigmoid(acc)
        elif j == 3:
            res = _gelu(acc)
        elif j == 4:
            a = _gelu(acc)
            mu = jnp.mean(a, axis=-1, keepdims=True)
            ac = a - mu
            var = jnp.mean(ac * ac, axis=-1, keepdims=True)
            res = ac * lax.rsqrt(var + EPS) * lng_ref[...] + lnb_ref[...]
        else:
            res = jax.nn.sigmoid(acc)
        o_ref[:, j * d:(j + 1) * d] = res.astype(BF16)


def _inproj(x2d, mod3, g, cos_t, sin_t, ln_g, ln_b, w_bf, seq):
    n, d = x2d.shape
    tm = TM_PROJ
    tpb = seq // tm
    return pl.pallas_call(
        _inproj_kernel,
        grid=(n // tm,),
        in_specs=[pl.BlockSpec((tm, d), lambda i: (i, 0)),
                  pl.BlockSpec((1, 1, d), lambda i: (i // tpb, 0, 0)),
                  pl.BlockSpec((1, 1, d), lambda i: (i // tpb, 0, 1)),
                  pl.BlockSpec((1, d), lambda i: (0, 0)),
                  pl.BlockSpec((tm, RET_QK_DIM), lambda i: (i % tpb, 0)),
                  pl.BlockSpec((tm, RET_QK_DIM), lambda i: (i % tpb, 0)),
                  pl.BlockSpec((1, d), lambda i: (0, 0)),
                  pl.BlockSpec((1, d), lambda i: (0, 0)),
                  pl.BlockSpec((d, IN_WIDTH), lambda i: (0, 0), pipeline_mode=pl.Buffered(1))],
        out_specs=pl.BlockSpec((tm, IN_WIDTH), lambda i: (i, 0)),
        out_shape=jax.ShapeDtypeStruct((n, IN_WIDTH), BF16),
        compiler_params=_cparams(("arbitrary",)),
        name="inproj",
    )(x2d, mod3, mod3, g, cos_t, sin_t, ln_g, ln_b, w_bf)


def _chunk_kernel(qk_ref, v_ref, gs_ref, u_ref, vn_ref, din_ref, dq_ref, dk_ref, dc_ref,
                  gn_ref, ws_ref, bs_ref, ya_ref, ys_ref, state_ref):
    @pl.when(pl.program_id(1) == 0)
    def _():
        state_ref[...] = jnp.zeros_like(state_ref)

    row = lax.broadcasted_iota(jnp.int32, (CHUNK, CHUNK), 0)
    col = lax.broadcasted_iota(jnp.int32, (CHUNK, CHUNK), 1)
    ws = [jnp.where(row >= col, ws_ref[gg], 0.0).astype(BF16) for gg in range(SGU_GROUPS)]

    for b in range(qk_ref.shape[0]):
        for hh in range(RET_HEADS):
            q = qk_ref[b, :, hh * RET_QK_DIM:(hh + 1) * RET_QK_DIM]
            k = qk_ref[b, :, (RET_HEADS + hh) * RET_QK_DIM:(RET_HEADS + hh + 1) * RET_QK_DIM]
            v = v_ref[b, :, hh * RET_V_DIM:(hh + 1) * RET_V_DIM]
            state = state_ref[b, hh]
            s = lax.dot_general(q, k, (((1,), (1,)), ((), ())), preferred_element_type=F32) * din_ref[hh]
            inner = jnp.dot(s.astype(BF16), v, preferred_element_type=F32)
            cross = jnp.dot(q, state.astype(BF16), preferred_element_type=F32) * dq_ref[hh]
            kd_t = (k.astype(F32) * dk_ref[hh]).T.astype(BF16)
            state_ref[b, hh] = state * dc_ref[hh] + jnp.dot(kd_t, v, preferred_element_type=F32)
            y = inner + cross
            mu = jnp.mean(y, axis=-1, keepdims=True)
            yc = y - mu
            var = jnp.mean(yc * yc, axis=-1, keepdims=True)
            sl = slice(hh * RET_V_DIM, (hh + 1) * RET_V_DIM)
            yn = yc * lax.rsqrt(var + EPS) * gn_ref[:, sl]
            ya_ref[b, :, sl] = (gs_ref[b, :, sl].astype(F32) * yn).astype(BF16)

        for gg in range(SGU_GROUPS):
            sl = slice(gg * SGU_GROUP_DIM, (gg + 1) * SGU_GROUP_DIM)
            mixed = jnp.dot(ws[gg], vn_ref[b, :, sl], preferred_element_type=F32) + bs_ref[gg]
            ys_ref[b, :, sl] = (u_ref[b, :, sl].astype(F32) * mixed).astype(BF16)


def _chunk(proj, din, dq, dk, dc, gn, ws, bs, bsz, seq):
    n = proj.shape[0]
    nc = seq // CHUNK
    d = D_MODEL
    nb = math.gcd(bsz, CHUNK_BATCHES)
    proj3 = proj.reshape(bsz, seq, IN_WIDTH)

    def col(j):
        return pl.BlockSpec((nb, CHUNK, d), lambda b, c: (b, c, j))

    def const(shape):
        return pl.BlockSpec(shape, lambda b, c: (0,) * len(shape))

    ya, ys = pl.pallas_call(
        _chunk_kernel,
        grid=(bsz // nb, nc),
        in_specs=[col(0), col(1), col(2), col(3), col(4),
                  const(din.shape), const(dq.shape), const(dk.shape), const(dc.shape),
                  const(gn.shape), const(ws.shape), const(bs.shape)],
        out_specs=[col(0), col(0)],
        out_shape=[jax.ShapeDtypeStruct((bsz, seq, d), BF16), jax.ShapeDtypeStruct((bsz, seq, d), BF16)],
        scratch_shapes=[pltpu.VMEM((nb, RET_HEADS, RET_QK_DIM, RET_V_DIM), F32)],
        compiler_params=_cparams(("arbitrary", "arbitrary")),
        name="chunk",
    )(proj3, proj3, proj3, proj3, proj3, din, dq, dk, dc, gn, ws, bs)
    return ya.reshape(n, d), ys.reshape(n, d)


def _merge_kernel(ya_ref, ys_ref, sa_ref, sb_ref, x_ref, g1_ref, sh2_ref, sc2_ref, n2_ref,
                  wa_ref, wb_ref, wo_ref, x1_ref, h2t_ref):
    ba = jnp.dot(ya_ref[...], wa_ref[...], preferred_element_type=F32)
    bb = jnp.dot(ys_ref[...], wb_ref[...], preferred_element_type=F32)
    merged = sa_ref[...].astype(F32) * ba + sb_ref[...].astype(F32) * bb
    mo = jnp.dot(merged.astype(BF16), wo_ref[...], preferred_element_type=F32)
    x1 = x_ref[...] + g1_ref[0] * mo
    x1_ref[...] = x1
    h2t_ref[...] = (_rms(x1, n2_ref[...]) * (1.0 + sc2_ref[0]) + sh2_ref[0]).T.astype(BF16)


def _merge(ya, ys, proj, x2d, mod3, n2, wa, wb, wo, seq):
    n, d = x2d.shape
    tm = TM_PROJ
    tpb = seq // tm

    def modspec(k):
        return pl.BlockSpec((1, 1, d), lambda i: (i // tpb, 0, k))

    def wspec():
        return pl.BlockSpec((d, d), lambda i: (0, 0))

    return pl.pallas_call(
        _merge_kernel,
        grid=(n // tm,),
        in_specs=[pl.BlockSpec((tm, d), lambda i: (i, 0)),
                  pl.BlockSpec((tm, d), lambda i: (i, 0)),
                  pl.BlockSpec((tm, d), lambda i: (i, 5)),
                  pl.BlockSpec((tm, d), lambda i: (i, 6)),
                  pl.BlockSpec((tm, d), lambda i: (i, 0)),
                  modspec(2), modspec(3), modspec(4),
                  pl.BlockSpec((1, d), lambda i: (0, 0)),
                  wspec(), wspec(), wspec()],
        out_specs=[pl.BlockSpec((tm, d), lambda i: (i, 0)),
                   pl.BlockSpec((d, tm), lambda i: (0, i))],
        out_shape=[jax.ShapeDtypeStruct((n, d), F32), jax.ShapeDtypeStruct((d, n), BF16)],
        compiler_params=_cparams(("arbitrary",)),
        name="merge",
    )(ya, ys, proj, proj, x2d, mod3, mod3, mod3, n2, wa, wb, wo)


def _sort_network(n):
    pairs = []
    p = 1
    while p < n:
        k = p
        while k >= 1:
            for j in range(k % p, n - k, 2 * k):
                for i in range(min(k, n - j - k)):
                    if (i + j) // (2 * p) == (i + j + k) // (2 * p):
                        pairs.append((i + j, i + j + k))
            k //= 2
        p *= 2
    return pairs


def _sort_desc(vs):
    vs = list(vs)
    for i, j in _sort_network(PEER_TOPK):
        if j < len(vs):
            vs[i], vs[j] = jnp.maximum(vs[i], vs[j]), jnp.minimum(vs[i], vs[j])
    return vs


def _merge_top16_over_sublanes(vs):
    n = PEER_TOPK
    vs = list(vs) + [None] * (n - len(vs))
    for shift in (4, 2, 1):
        w = []
        for k in range(n):
            a, b = vs[k], vs[n - 1 - k]
            b = None if b is None else pltpu.roll(b, shift, 0)
            w.append(a if b is None else (b if a is None else jnp.maximum(a, b)))
        d = n // 2
        while d >= 1:
            for k in range(n):
                if k & d == 0:
                    w[k], w[k + d] = jnp.maximum(w[k], w[k + d]), jnp.minimum(w[k], w[k + d])
            d //= 2
        vs = w
    return vs


def _prefix_count(a, pred):
    assert len(a) == PEER_TOPK == 16
    m1 = pred(a[7])
    m2 = pred(jnp.where(m1, a[11], a[3]))
    m3 = pred(jnp.where(m1, jnp.where(m2, a[13], a[9]), jnp.where(m2, a[5], a[1])))
    hi = jnp.where(m2, jnp.where(m3, a[14], a[12]), jnp.where(m3, a[10], a[8]))
    lo = jnp.where(m2, jnp.where(m3, a[6], a[4]), jnp.where(m3, a[2], a[0]))
    m4 = pred(jnp.where(m1, hi, lo))
    c = (jnp.where(m1, 8.0, 0.0) + jnp.where(m2, 4.0, 0.0)) + (jnp.where(m3, 2.0, 0.0) + jnp.where(m4, 1.0, 0.0))
    return jnp.where(pred(a[15]), 16.0, c)


def _topk_kernel(h2t_ref, wq_ref, keys_ref, cnt_ref, e1_ref, rank_ref, e2_ref, qt_ref):
    nk = PEER_N_KEYS
    ntb = TOPK_TOKENS // LANES
    nv = nk // SUBLANES
    qt_ref[...] = jnp.dot(wq_ref[...], h2t_ref[...], preferred_element_type=F32).astype(BF16)

    def pack(rep, sub_iota):
        out = rep[SUBLANES - 1]
        for r in range(SUBLANES - 2, -1, -1):
            out = jnp.where(sub_iota == r, rep[r], out)
        return out

    def head_body(hh, carry):
        sub_iota = lax.broadcasted_iota(jnp.int32, (SUBLANES, LANES), 0)
        sts = []
        for p in range(2):
            off = pl.multiple_of((hh * 2 + p) * nk, nk)
            sts.append(jnp.dot(keys_ref[hh * 2 + p], qt_ref[pl.ds(off, nk), :],
                               preferred_element_type=F32))
        for tb in range(ntb):
            lsl = slice(tb * LANES, (tb + 1) * LANES)
            rows = [[sts[p][r * SUBLANES:(r + 1) * SUBLANES, lsl] for r in range(nv)] for p in range(2)]
            a1 = _merge_top16_over_sublanes(_sort_desc(rows[0]))
            a2 = _merge_top16_over_sublanes(_sort_desc(rows[1]))
            a1lo, a1hi = pack(a1[:SUBLANES], sub_iota), pack(a1[SUBLANES:], sub_iota)
            a2lo, a2hi = pack(a2[:SUBLANES], sub_iota), pack(a2[SUBLANES:], sub_iota)
            tail = sub_iota >= 2
            cands = [a1lo + a2[0], a1hi + a2[0], a1lo + a2[1],
                     jnp.where(tail, a1[0] + a2lo, NEG_INF), a1[0] + a2hi,
                     jnp.where(tail, a1[1] + a2lo, NEG_INF)]
            cands += [jnp.where(tail, a1lo + a2[l], NEG_INF) for l in (2, 3, 4)]
            top = _merge_top16_over_sublanes(_sort_desc(cands))
            tau = top[PEER_TOPK - 1]
            z = functools.reduce(lambda x, y: x + y, [jnp.exp(t - top[0]) for t in top])
            rz = 1.0 / z
            for rp in range(nv // 2):
                ranks, e2s = [], []
                for r in (2 * rp, 2 * rp + 1):
                    rsl = slice(r * SUBLANES, (r + 1) * SUBLANES)
                    s1v, s2v = rows[0][r], rows[1][r]
                    cnt = _prefix_count(a2, lambda t: s1v + t >= tau)
                    rank = _prefix_count(a2, lambda t: t > s2v)
                    cnt_ref[tb, hh, rsl, :] = cnt
                    e1_ref[tb, hh, rsl, :] = jnp.exp(s1v - a1[0]) * rz
                    ranks.append(rank)
                    e2s.append(jnp.exp(s2v - a2[0]))
                psl = slice(rp * SUBLANES, (rp + 1) * SUBLANES)
                rank_ref[tb, hh, psl, :] = pltpu.bitcast(jnp.concatenate(ranks, axis=0).astype(BF16), jnp.uint32)
                e2_ref[tb, hh, psl, :] = pltpu.bitcast(jnp.concatenate(e2s, axis=0).astype(BF16), jnp.uint32)
        return carry

    lax.fori_loop(0, PEER_HEADS, head_body, 0)


def _topk(h2t, wq_t, keys):
    d, n = h2t.shape
    tt = TOPK_TOKENS
    ntb = tt // LANES
    nb = n // LANES
    shape = (nb, PEER_HEADS, PEER_N_KEYS, LANES)
    pshape = (nb, PEER_HEADS, PEER_N_KEYS // 2, LANES)
    bigspec = pl.BlockSpec((ntb,) + shape[1:], lambda i: (i, 0, 0, 0))
    pspec = pl.BlockSpec((ntb,) + pshape[1:], lambda i: (i, 0, 0, 0))
    return pl.pallas_call(
        _topk_kernel,
        grid=(n // tt,),
        in_specs=[pl.BlockSpec((d, tt), lambda i: (0, i)),
                  pl.BlockSpec(wq_t.shape, lambda i: (0, 0)),
                  pl.BlockSpec(keys.shape, lambda i: (0, 0, 0))],
        out_specs=[bigspec, bigspec, pspec, pspec],
        out_shape=[jax.ShapeDtypeStruct(shape, F32), jax.ShapeDtypeStruct(shape, F32),
                   jax.ShapeDtypeStruct(pshape, jnp.uint32), jax.ShapeDtypeStruct(pshape, jnp.uint32)],
        scratch_shapes=[pltpu.VMEM((wq_t.shape[0], tt), BF16)],
        compiler_params=_cparams(("arbitrary",)),
        name="topk",
    )(h2t, wq_t, keys)


def _peer_kernel(h2t_ref, u_ref, vt_ref, cnt_ref, e1_ref, rank_ref, e2_ref, x1_ref, g2_ref, fg_ref,
                 o_ref, acc_ref, st_ref, a_ref):
    e = pl.program_id(1)
    ntb = PEER_TOKENS // LANES
    nk = PEER_N_KEYS
    npass = PEER_GROUP // PEER_PASS
    ni = PEER_PASS // nk
    nrow = nk // PACKED_ROWS
    nq = PEER_EXPERTS // PEER_GROUP

    @pl.when(e == 0)
    def _():
        acc_ref[...] = jnp.zeros_like(acc_ref)

    def packed_row(ref, tb, hh, key):
        return jnp.broadcast_to(ref[tb, hh, key:key + 1, :], (PACKED_ROWS, LANES)).astype(BF16)

    def gate_pass(g, ps):
        key0 = g * (PEER_GROUP // nk) + ps * ni
        rbase = ps * PEER_PASS
        for tb in range(ntb):
            lsl = slice(tb * LANES, (tb + 1) * LANES)
            gs = [[jnp.zeros((PACKED_ROWS, LANES), BF16) for _ in range(nrow)] for _ in range(ni)]
            for hh in range(PEER_HEADS):
                cnts = [packed_row(cnt_ref, tb, hh, key0 + ii) for ii in range(ni)]
                e1s = [packed_row(e1_ref, tb, hh, key0 + ii) for ii in range(ni)]
                for r in range(nrow):
                    wsl = slice(r * SUBLANES, (r + 1) * SUBLANES)
                    rk = pltpu.bitcast(rank_ref[tb, hh, wsl, :], BF16)
                    e2v = pltpu.bitcast(e2_ref[tb, hh, wsl, :], BF16)
                    for ii in range(ni):
                        gs[ii][r] = gs[ii][r] + jnp.where(rk < cnts[ii], e2v, jnp.zeros_like(e2v)) * e1s[ii]
            for ii in range(ni):
                for r in range(nrow):
                    r0 = rbase + ii * nk + r * PACKED_ROWS
                    rows = slice(r0, r0 + PACKED_ROWS)
                    a_ref[rows, lsl] = _gelu(st_ref[rows, lsl].astype(BF16)) * gs[ii][r]

    def gates(g):
        for ps in range(npass):
            pl.when(e >= 0)(functools.partial(gate_pass, g, ps))

    for g in range(nq):
        gsl = slice(g * PEER_GROUP, (g + 1) * PEER_GROUP)
        st_ref[...] = jnp.dot(u_ref[gsl, :], h2t_ref[...], preferred_element_type=F32)
        gates(g)
        acc_ref[...] += jnp.dot(vt_ref[:, gsl], a_ref[...], preferred_element_type=F32)

    @pl.when(e == pl.num_programs(1) - 1)
    def _():
        x2 = x1_ref[...] + g2_ref[0] * acc_ref[...].T
        o_ref[...] = _rms(x2, fg_ref[...])


def _peer(h2t, u_bf, vt_bf, cnt, e1, rank, e2, x1, mod3, fg, seq):
    d, n = h2t.shape
    tt = PEER_TOKENS
    eb = PEER_EXPERTS
    gq = PEER_GROUP
    ntb = tt // LANES
    tpb = seq // tt
    n_exp = u_bf.shape[0]
    rowspec = pl.BlockSpec((ntb, PEER_HEADS, eb // PEER_N_KEYS, LANES), lambda t, e: (t, 0, e, 0))
    pspec = pl.BlockSpec((ntb, PEER_HEADS, PEER_N_KEYS // 2, LANES), lambda t, e: (t, 0, 0, 0))
    assert eb // PEER_N_KEYS == SUBLANES
    return pl.pallas_call(
        _peer_kernel,
        grid=(n // tt, n_exp // eb),
        in_specs=[pl.BlockSpec((d, tt), lambda t, e: (0, t)),
                  pl.BlockSpec((eb, d), lambda t, e: (e, 0)),
                  pl.BlockSpec((d, eb), lambda t, e: (0, e)),
                  rowspec, rowspec, pspec, pspec,
                  pl.BlockSpec((tt, d), lambda t, e: (t, 0)),
                  pl.BlockSpec((1, 1, d), lambda t, e: (t // tpb, 0, 5)),
                  pl.BlockSpec((1, d), lambda t, e: (0, 0))],
        out_specs=pl.BlockSpec((tt, d), lambda t, e: (t, 0)),
        out_shape=jax.ShapeDtypeStruct((n, d), F32),
        scratch_shapes=[pltpu.VMEM((d, tt), F32),
                        pltpu.VMEM((gq, tt), F32),
                        pltpu.VMEM((gq, tt), BF16)],
        compiler_params=_cparams(("arbitrary", "arbitrary")),
        name="peer",
    )(h2t, u_bf, vt_bf, cnt, e1, rank, e2, x1, mod3, fg)


def _retention_tables():
    hcount = RET_HEADS
    c = CHUNK
    gamma = 1.0 - 2.0 ** (-5.0 - jnp.arange(hcount, dtype=F32))
    log_g = jnp.log(gamma)
    idx = jnp.arange(c, dtype=F32)
    diff = idx[:, None] - idx[None, :]
    din = jnp.where((diff >= 0)[None], jnp.exp(log_g[:, None, None] * jnp.maximum(diff, 0.0)[None]), 0.0)
    dq = jnp.exp(log_g[:, None] * (idx[None, :] + 1.0))
    dk = jnp.exp(log_g[:, None] * (c - 1.0 - idx[None, :]))
    dc = jnp.exp(log_g * c)
    dq = jnp.broadcast_to(dq[:, :, None], (hcount, c, RET_V_DIM))
    dk = jnp.broadcast_to(dk[:, :, None], (hcount, c, RET_QK_DIM))
    dc = jnp.broadcast_to(dc[:, None, None], (hcount, RET_QK_DIM, RET_V_DIM))
    return din.astype(F32), dq.astype(F32), dk.astype(F32), dc.astype(F32)


def _rotary_tables(seq):
    half = RET_QK_DIM // 2
    pos = jnp.arange(seq, dtype=F32)
    inv = ROPE_BASE ** (-jnp.arange(half, dtype=F32) * 2.0 / RET_QK_DIM)
    ang = pos[:, None] * inv[None, :]
    cos = jnp.cos(ang)
    sin = jnp.sin(ang)
    return jnp.concatenate([cos, cos], axis=-1), jnp.concatenate([-sin, sin], axis=-1)


def kernel(x, c, w_ada, b_ada, norm1_g, w_in, ret_gn_g, sgu_ln_g, sgu_ln_b, sgu_w, sgu_b, w_ret_out,
           w_sgu_out, w_out, norm2_g, peer_w_q, peer_sub_keys, peer_u, peer_v, final_g):
    bsz, seq, d = x.shape
    n = bsz * seq
    depth = w_ada.shape[0]
    assert d == D_MODEL and seq % TM_PROJ == 0 and seq % PEER_TOKENS == 0 and seq % TOPK_TOKENS == 0
    cos_t, sin_t = _rotary_tables(seq)
    din, dq, dk, dc = _retention_tables()
    xc = x.reshape(n, d)
    for l in range(depth):
        mod3 = _ada(c, w_ada[l], b_ada[l]).reshape(bsz, 1, N_MOD * d)
        proj = _inproj(xc, mod3, norm1_g[l].reshape(1, d), cos_t, sin_t,
                       sgu_ln_g[l].reshape(1, d), sgu_ln_b[l].reshape(1, d), w_in[l].astype(BF16), seq)
        bs = jnp.broadcast_to(sgu_b[l][:, :, None], (SGU_GROUPS, CHUNK, SGU_GROUP_DIM))
        ya, ys = _chunk(proj, din, dq, dk, dc, ret_gn_g[l].reshape(1, d), sgu_w[l], bs, bsz, seq)
        x1, h2t = _merge(ya, ys, proj, xc, mod3, norm2_g[l].reshape(1, d), w_ret_out[l].astype(BF16),
                        w_sgu_out[l].astype(BF16), w_out[l].astype(BF16), seq)
        wq_t = peer_w_q[l].T.astype(BF16)
        keys = peer_sub_keys[l].reshape(PEER_HEADS * 2, PEER_N_KEYS, -1).astype(BF16)
        cnt, e1, rank, e2 = _topk(h2t, wq_t, keys)
        assert depth == 1
        xc = _peer(h2t, peer_u[l].astype(BF16), peer_v[l].T.astype(BF16), cnt, e1, rank, e2, x1, mod3,
                   final_g.reshape(1, d), seq)
    return xc.reshape(bsz, seq, d)
```

```python
import functools
import math

import jax
import jax.numpy as jnp
from jax import lax
from jax.experimental import pallas as pl
from jax.experimental.pallas import tpu as pltpu

F32 = jnp.float32
BF16 = jnp.bfloat16

D_MODEL = 1024
RET_HEADS = 4
RET_QK_DIM = 128
RET_V_DIM = 256
CHUNK = 128
SGU_GROUPS = 4
SGU_GROUP_DIM = 256
IN_WIDTH = 7168
PEER_HEADS = 8
PEER_N_KEYS = 128
PEER_TOPK = 16
N_MOD = 6
EPS = 1e-6
ROPE_BASE = 10000.0
NEG_INF = float("-inf")

VMEM_LIMIT_BYTES = 56 * 1024 * 1024

TM_PROJ = 512
CHUNK_BATCHES = 8
TOPK_TOKENS = 1024
PEER_TOKENS = 1024
PEER_EXPERTS = 1024
PEER_GROUP = 1024
PEER_PASS = 256
LANES = 128
SUBLANES = 8
PACKED_ROWS = 16


def _gelu(x):
    return 0.5 * x * (1.0 + lax.erf(x * (1.0 / math.sqrt(2.0))))


def _rms(x, g):
    ms = jnp.mean(x * x, axis=-1, keepdims=True)
    return x * lax.rsqrt(ms + EPS) * g


def _cparams(sem):
    return pltpu.CompilerParams(dimension_semantics=sem, vmem_limit_bytes=VMEM_LIMIT_BYTES)


def _ada_kernel(c_ref, w_ref, b_ref, o_ref):
    c = c_ref[...]
    ca = c * jax.nn.sigmoid(c)
    o_ref[...] = jnp.dot(ca.astype(BF16), w_ref[...].astype(BF16),
                         preferred_element_type=F32) + b_ref[...]


def _ada(c, w, b):
    bsz, d = c.shape
    n_out = w.shape[1]
    return pl.pallas_call(
        _ada_kernel,
        grid=(n_out // d,),
        in_specs=[pl.BlockSpec((bsz, d), lambda j: (0, 0)),
                  pl.BlockSpec((d, d), lambda j: (0, j)),
                  pl.BlockSpec((1, d), lambda j: (0, j))],
        out_specs=pl.BlockSpec((bsz, d), lambda j: (0, j)),
        out_shape=jax.ShapeDtypeStruct((bsz, n_out), F32),
        compiler_params=_cparams(("arbitrary",)),
        name="ada",
    )(c, w, b.reshape(1, n_out))


def _inproj_kernel(x_ref, sh_ref, sc_ref, g_ref, cos_ref, sin_ref, lng_ref, lnb_ref, w_ref, o_ref):
    h = (_rms(x_ref[...], g_ref[...]) * (1.0 + sc_ref[0]) + sh_ref[0]).astype(BF16)
    d = D_MODEL
    for j in range(IN_WIDTH // d):
        acc = jnp.dot(h, w_ref[:, j * d:(j + 1) * d], preferred_element_type=F32)
        if j == 0:
            cos = cos_ref[...]
            sin = sin_ref[...]
            for hh in range(2 * RET_HEADS):
                blk = acc[:, hh * RET_QK_DIM:(hh + 1) * RET_QK_DIM]
                r = blk * cos + pltpu.roll(blk, RET_QK_DIM // 2, 1) * sin
                if hh >= RET_HEADS:
                    r = r * (RET_QK_DIM ** -0.5)
                o_ref[:, hh * RET_QK_DIM:(hh + 1) * RET_QK_DIM] = r.astype(BF16)
            continue
        if j == 1:
            res = acc
        elif j == 2:
            res = acc * jax.nn.sigmoid(acc)
        elif j == 3:
            res = _gelu(acc)
        elif j == 4:
            a = _gelu(acc)
            mu = jnp.mean(a, axis=-1, keepdims=True)
            ac = a - mu
            var = jnp.mean(ac * ac, axis=-1, keepdims=True)
            res = ac * lax.rsqrt(var + EPS) * lng_ref[...] + lnb_ref[...]
        else:
            res = jax.nn.sigmoid(acc)
        o_ref[:, j * d:(j + 1) * d] = res.astype(BF16)


def _inproj(x2d, mod3, g, cos_t, sin_t, ln_g, ln_b, w_bf, seq):
    n, d = x2d.shape
    tm = TM_PROJ
    tpb = seq // tm
    return pl.pallas_call(
        _inproj_kernel,
        grid=(n // tm,),
        in_specs=[pl.BlockSpec((tm, d), lambda i: (i, 0)),
                  pl.BlockSpec((1, 1, d), lambda i: (i // tpb, 0, 0)),
                  pl.BlockSpec((1, 1, d), lambda i: (i // tpb, 0, 1)),
                  pl.BlockSpec((1, d), lambda i: (0, 0)),
                  pl.BlockSpec((tm, RET_QK_DIM), lambda i: (i % tpb, 0)),
                  pl.BlockSpec((tm, RET_QK_DIM), lambda i: (i % tpb, 0)),
                  pl.BlockSpec((1, d), lambda i: (0, 0)),
                  pl.BlockSpec((1, d), lambda i: (0, 0)),
                  pl.BlockSpec((d, IN_WIDTH), lambda i: (0, 0), pipeline_mode=pl.Buffered(1))],
        out_specs=pl.BlockSpec((tm, IN_WIDTH), lambda i: (i, 0)),
        out_shape=jax.ShapeDtypeStruct((n, IN_WIDTH), BF16),
        compiler_params=_cparams(("arbitrary",)),
        name="inproj",
    )(x2d, mod3, mod3, g, cos_t, sin_t, ln_g, ln_b, w_bf)


def _chunk_kernel(qk_ref, v_ref, gs_ref, u_ref, vn_ref, din_ref, dq_ref, dk_ref, dc_ref,
                  gn_ref, ws_ref, bs_ref, ya_ref, ys_ref, state_ref):
    @pl.when(pl.program_id(1) == 0)
    def _():
        state_ref[...] = jnp.zeros_like(state_ref)

    row = lax.broadcasted_iota(jnp.int32, (CHUNK, CHUNK), 0)
    col = lax.broadcasted_iota(jnp.int32, (CHUNK, CHUNK), 1)
    ws = [jnp.where(row >= col, ws_ref[gg], 0.0).astype(BF16) for gg in range(SGU_GROUPS)]

    for b in range(qk_ref.shape[0]):
        for hh in range(RET_HEADS):
            q = qk_ref[b, :, hh * RET_QK_DIM:(hh + 1) * RET_QK_DIM]
            k = qk_ref[b, :, (RET_HEADS + hh) * RET_QK_DIM:(RET_HEADS + hh + 1) * RET_QK_DIM]
            v = v_ref[b, :, hh * RET_V_DIM:(hh + 1) * RET_V_DIM]
            state = state_ref[b, hh]
            s = lax.dot_general(q, k, (((1,), (1,)), ((), ())), preferred_element_type=F32) * din_ref[hh]
            inner = jnp.dot(s.astype(BF16), v, preferred_element_type=F32)
            cross = jnp.dot(q, state.astype(BF16), preferred_element_type=F32) * dq_ref[hh]
            kd_t = (k.astype(F32) * dk_ref[hh]).T.astype(BF16)
            state_ref[b, hh] = state * dc_ref[hh] + jnp.dot(kd_t, v, preferred_element_type=F32)
            y = inner + cross
            mu = jnp.mean(y, axis=-1, keepdims=True)
            yc = y - mu
            var = jnp.mean(yc * yc, axis=-1, keepdims=True)
            sl = slice(hh * RET_V_DIM, (hh + 1) * RET_V_DIM)
            yn = yc * lax.rsqrt(var + EPS) * gn_ref[:, sl]
            ya_ref[b, :, sl] = (gs_ref[b, :, sl].astype(F32) * yn).astype(BF16)

        for gg in range(SGU_GROUPS):
            sl = slice(gg * SGU_GROUP_DIM, (gg + 1) * SGU_GROUP_DIM)
            mixed = jnp.dot(ws[gg], vn_ref[b, :, sl], preferred_element_type=F32) + bs_ref[gg]
            ys_ref[b, :, sl] = (u_ref[b, :, sl].astype(F32) * mixed).astype(BF16)


def _chunk(proj, din, dq, dk, dc, gn, ws, bs, bsz, seq):
    n = proj.shape[0]
    nc = seq // CHUNK
    d = D_MODEL
    nb = math.gcd(bsz, CHUNK_BATCHES)
    proj3 = proj.reshape(bsz, seq, IN_WIDTH)

    def col(j):
        return pl.BlockSpec((nb, CHUNK, d), lambda b, c: (b, c, j))

    def const(shape):
        return pl.BlockSpec(shape, lambda b, c: (0,) * len(shape))

    ya, ys = pl.pallas_call(
        _chunk_kernel,
        grid=(bsz // nb, nc),
        in_specs=[col(0), col(1), col(2), col(3), col(4),
                  const(din.shape), const(dq.shape), const(dk.shape), const(dc.shape),
                  const(gn.shape), const(ws.shape), const(bs.shape)],
        out_specs=[col(0), col(0)],
        out_shape=[jax.ShapeDtypeStruct((bsz, seq, d), BF16), jax.ShapeDtypeStruct((bsz, seq, d), BF16)],
        scratch_shapes=[pltpu.VMEM((nb, RET_HEADS, RET_QK_DIM, RET_V_DIM), F32)],
        compiler_params=_cparams(("arbitrary", "arbitrary")),
        name="chunk",
    )(proj3, proj3, proj3, proj3, proj3, din, dq, dk, dc, gn, ws, bs)
    return ya.reshape(n, d), ys.reshape(n, d)


def _merge_kernel(ya_ref, ys_ref, sa_ref, sb_ref, x_ref, g1_ref, sh2_ref, sc2_ref, n2_ref,
                  wa_ref, wb_ref, wo_ref, x1_ref, h2t_ref):
    ba = jnp.dot(ya_ref[...], wa_ref[...], preferred_element_type=F32)
    bb = jnp.dot(ys_ref[...], wb_ref[...], preferred_element_type=F32)
    merged = sa_ref[...].astype(F32) * ba + sb_ref[...].astype(F32) * bb
    mo = jnp.dot(merged.astype(BF16), wo_ref[...], preferred_element_type=F32)
    x1 = x_ref[...] + g1_ref[0] * mo
    x1_ref[...] = x1
    h2t_ref[...] = (_rms(x1, n2_ref[...]) * (1.0 + sc2_ref[0]) + sh2_ref[0]).T.astype(BF16)


def _merge(ya, ys, proj, x2d, mod3, n2, wa, wb, wo, seq):
    n, d = x2d.shape
    tm = TM_PROJ
    tpb = seq // tm

    def modspec(k):
        return pl.BlockSpec((1, 1, d), lambda i: (i // tpb, 0, k))

    def wspec():
        return pl.BlockSpec((d, d), lambda i: (0, 0))

    return pl.pallas_call(
        _merge_kernel,
        grid=(n // tm,),
        in_specs=[pl.BlockSpec((tm, d), lambda i: (i, 0)),
                  pl.BlockSpec((tm, d), lambda i: (i, 0)),
                  pl.BlockSpec((tm, d), lambda i: (i, 5)),
                  pl.BlockSpec((tm, d), lambda i: (i, 6)),
                  pl.BlockSpec((tm, d), lambda i: (i, 0)),
                  modspec(2), modspec(3), modspec(4),
                  pl.BlockSpec((1, d), lambda i: (0, 0)),
                  wspec(), wspec(), wspec()],
        out_specs=[pl.BlockSpec((tm, d), lambda i: (i, 0)),
                   pl.BlockSpec((d, tm), lambda i: (0, i))],
        out_shape=[jax.ShapeDtypeStruct((n, d), F32), jax.ShapeDtypeStruct((d, n), BF16)],
        compiler_params=_cparams(("arbitrary",)),
        name="merge",
    )(ya, ys, proj, proj, x2d, mod3, mod3, mod3, n2, wa, wb, wo)


def _sort_network(n):
    pairs = []
    p = 1
    while p < n:
        k = p
        while k >= 1:
            for j in range(k % p, n - k, 2 * k):
                for i in range(min(k, n - j - k)):
                    if (i + j) // (2 * p) == (i + j + k) // (2 * p):
                        pairs.append((i + j, i + j + k))
            k //= 2
        p *= 2
    return pairs


def _sort_desc(vs):
    vs = list(vs)
    for i, j in _sort_network(PEER_TOPK):
        if j < len(vs):
            vs[i], vs[j] = jnp.maximum(vs[i], vs[j]), jnp.minimum(vs[i], vs[j])
    return vs


def _merge_top16_over_sublanes(vs):
    n = PEER_TOPK
    vs = list(vs) + [None] * (n - len(vs))
    for shift in (4, 2, 1):
        w = []
        for k in range(n):
            a, b = vs[k], vs[n - 1 - k]
            b = None if b is None else pltpu.roll(b, shift, 0)
            w.append(a if b is None else (b if a is None else jnp.maximum(a, b)))
        d = n // 2
        while d >= 1:
            for k in range(n):
                if k & d == 0:
                    w[k], w[k + d] = jnp.maximum(w[k], w[k + d]), jnp.minimum(w[k], w[k + d])
            d //= 2
        vs = w
    return vs


def _prefix_count(a, pred):
    assert len(a) == PEER_TOPK == 16
    m1 = pred(a[7])
    m2 = pred(jnp.where(m1, a[11], a[3]))
    m3 = pred(jnp.where(m1, jnp.where(m2, a[13], a[9]), jnp.where(m2, a[5], a[1])))
    hi = jnp.where(m2, jnp.where(m3, a[14], a[12]), jnp.where(m3, a[10], a[8]))
    lo = jnp.where(m2, jnp.where(m3, a[6], a[4]), jnp.where(m3, a[2], a[0]))
    m4 = pred(jnp.where(m1, hi, lo))
    c = (jnp.where(m1, 8.0, 0.0) + jnp.where(m2, 4.0, 0.0)) + (jnp.where(m3, 2.0, 0.0) + jnp.where(m4, 1.0, 0.0))
    return jnp.where(pred(a[15]), 16.0, c)


def _topk_kernel(h2t_ref, wq_ref, keys_ref, cnt_ref, e1_ref, rank_ref, e2_ref, qt_ref):
    nk = PEER_N_KEYS
    ntb = TOPK_TOKENS // LANES
    nv = nk // SUBLANES
    qt_ref[...] = jnp.dot(wq_ref[...], h2t_ref[...], preferred_element_type=F32).astype(BF16)

    def pack(rep, sub_iota):
        out = rep[SUBLANES - 1]
        for r in range(SUBLANES - 2, -1, -1):
            out = jnp.where(sub_iota == r, rep[r], out)
        return out

    def head_body(hh, carry):
        sub_iota = lax.broadcasted_iota(jnp.int32, (SUBLANES, LANES), 0)
        sts = []
        for p in range(2):
            off = pl.multiple_of((hh * 2 + p) * nk, nk)
            sts.append(jnp.dot(keys_ref[hh * 2 + p], qt_ref[pl.ds(off, nk), :],
                               preferred_element_type=F32))
        for tb in range(ntb):
            lsl = slice(tb * LANES, (tb + 1) * LANES)
            rows = [[sts[p][r * SUBLANES:(r + 1) * SUBLANES, lsl] for r in range(nv)] for p in range(2)]
            a1 = _merge_top16_over_sublanes(_sort_desc(rows[0]))
            a2 = _merge_top16_over_sublanes(_sort_desc(rows[1]))
            a1lo, a1hi = pack(a1[:SUBLANES], sub_iota), pack(a1[SUBLANES:], sub_iota)
            a2lo, a2hi = pack(a2[:SUBLANES], sub_iota), pack(a2[SUBLANES:], sub_iota)
            tail = sub_iota >= 2
            cands = [a1lo + a2[0], a1hi + a2[0], a1lo + a2[1],
                     jnp.where(tail, a1[0] + a2lo, NEG_INF), a1[0] + a2hi,
                     jnp.where(tail, a1[1] + a2lo, NEG_INF)]
            cands += [jnp.where(tail, a1lo + a2[l], NEG_INF) for l in (2, 3, 4)]
            top = _merge_top16_over_sublanes(_sort_desc(cands))
            tau = top[PEER_TOPK - 1]
            z = functools.reduce(lambda x, y: x + y, [jnp.exp(t - top[0]) for t in top])
            rz = 1.0 / z
            for rp in range(nv // 2):
                ranks, e2s = [], []
                for r in (2 * rp, 2 * rp + 1):
                    rsl = slice(r * SUBLANES, (r + 1) * SUBLANES)
                    s1v, s2v = rows[0][r], rows[1][r]
                    cnt = _prefix_count(a2, lambda t: s1v + t >= tau)
                    rank = _prefix_count(a2, lambda t: t > s2v)
                    cnt_ref[tb, hh, rsl, :] = cnt
                    e1_ref[tb, hh, rsl, :] = jnp.exp(s1v - a1[0]) * rz
                    ranks.append(rank)
                    e2s.append(jnp.exp(s2v - a2[0]))
                psl = slice(rp * SUBLANES, (rp + 1) * SUBLANES)
                rank_ref[tb, hh, psl, :] = pltpu.bitcast(jnp.concatenate(ranks, axis=0).astype(BF16), jnp.uint32)
                e2_ref[tb, hh, psl, :] = pltpu.bitcast(jnp.concatenate(e2s, axis=0).astype(BF16), jnp.uint32)
        return carry

    lax.fori_loop(0, PEER_HEADS, head_body, 0)


def _topk(h2t, wq_t, keys):
    d, n = h2t.shape
    tt = TOPK_TOKENS
    ntb = tt // LANES
    nb = n // LANES
    shape = (nb, PEER_HEADS, PEER_N_KEYS, LANES)
    pshape = (nb, PEER_HEADS, PEER_N_KEYS // 2, LANES)
    bigspec = pl.BlockSpec((ntb,) + shape[1:], lambda i: (i, 0, 0, 0))
    pspec = pl.BlockSpec((ntb,) + pshape[1:], lambda i: (i, 0, 0, 0))
    return pl.pallas_call(
        _topk_kernel,
        grid=(n // tt,),
        in_specs=[pl.BlockSpec((d, tt), lambda i: (0, i)),
                  pl.BlockSpec(wq_t.shape, lambda i: (0, 0)),
                  pl.BlockSpec(keys.shape, lambda i: (0, 0, 0))],
        out_specs=[bigspec, bigspec, pspec, pspec],
        out_shape=[jax.ShapeDtypeStruct(shape, F32), jax.ShapeDtypeStruct(shape, F32),
                   jax.ShapeDtypeStruct(pshape, jnp.uint32), jax.ShapeDtypeStruct(pshape, jnp.uint32)],
        scratch_shapes=[pltpu.VMEM((wq_t.shape[0], tt), BF16)],
        compiler_params=_cparams(("arbitrary",)),
        name="topk",
    )(h2t, wq_t, keys)


def _peer_kernel(h2t_ref, u_ref, vt_ref, cnt_ref, e1_ref, rank_ref, e2_ref, x1_ref, g2_ref, fg_ref,
                 o_ref, acc_ref, st_ref, a_ref):
    e = pl.program_id(1)
    ntb = PEER_TOKENS // LANES
    nk = PEER_N_KEYS
    npass = PEER_GROUP // PEER_PASS
    ni = PEER_PASS // nk
    nrow = nk // PACKED_ROWS
    nq = PEER_EXPERTS // PEER_GROUP

    @pl.when(e == 0)
    def _():
        acc_ref[...] = jnp.zeros_like(acc_ref)

    def packed_row(ref, tb, hh, key):
        return jnp.broadcast_to(ref[tb, hh, key:key + 1, :], (PACKED_ROWS, LANES)).astype(BF16)

    def gate_pass(g, ps):
        key0 = g * (PEER_GROUP // nk) + ps * ni
        rbase = ps * PEER_PASS
        for tb in range(ntb):
            lsl = slice(tb * LANES, (tb + 1) * LANES)
            gs = [[jnp.zeros((PACKED_ROWS, LANES), BF16) for _ in range(nrow)] for _ in range(ni)]
            for hh in range(PEER_HEADS):
                cnts = [packed_row(cnt_ref, tb, hh, key0 + ii) for ii in range(ni)]
                e1s = [packed_row(e1_ref, tb, hh, key0 + ii) for ii in range(ni)]
                for r in range(nrow):
                    wsl = slice(r * SUBLANES, (r + 1) * SUBLANES)
                    rk = pltpu.bitcast(rank_ref[tb, hh, wsl, :], BF16)
                    e2v = pltpu.bitcast(e2_ref[tb, hh, wsl, :], BF16)
                    for ii in range(ni):
                        gs[ii][r] = gs[ii][r] + jnp.where(rk < cnts[ii], e2v, jnp.zeros_like(e2v)) * e1s[ii]
            for ii in range(ni):
                for r in range(nrow):
                    r0 = rbase + ii * nk + r * PACKED_ROWS
                    rows = slice(r0, r0 + PACKED_ROWS)
                    a_ref[rows, lsl] = _gelu(st_ref[rows, lsl].astype(BF16)) * gs[ii][r]

    def gates(g):
        for ps in range(npass):
            pl.when(e >= 0)(functools.partial(gate_pass, g, ps))

    for g in range(nq):
        gsl = slice(g * PEER_GROUP, (g + 1) * PEER_GROUP)
        st_ref[...] = jnp.dot(u_ref[gsl, :], h2t_ref[...], preferred_element_type=F32)
        gates(g)
        acc_ref[...] += jnp.dot(vt_ref[:, gsl], a_ref[...], preferred_element_type=F32)

    @pl.when(e == pl.num_programs(1) - 1)
    def _():
        x2 = x1_ref[...] + g2_ref[0] * acc_ref[...].T
        o_ref[...] = _rms(x2, fg_ref[...])


def _peer(h2t, u_bf, vt_bf, cnt, e1, rank, e2, x1, mod3, fg, seq):
    d, n = h2t.shape
    tt = PEER_TOKENS
    eb = PEER_EXPERTS
    gq = PEER_GROUP
    ntb = tt // LANES
    tpb = seq // tt
    n_exp = u_bf.shape[0]
    rowspec = pl.BlockSpec((ntb, PEER_HEADS, eb // PEER_N_KEYS, LANES), lambda t, e: (t, 0, e, 0))
    pspec = pl.BlockSpec((ntb, PEER_HEADS, PEER_N_KEYS // 2, LANES), lambda t, e: (t, 0, 0, 0))
    assert eb // PEER_N_KEYS == SUBLANES
    return pl.pallas_call(
        _peer_kernel,
        grid=(n // tt, n_exp // eb),
        in_specs=[pl.BlockSpec((d, tt), lambda t, e: (0, t)),
                  pl.BlockSpec((eb, d), lambda t, e: (e, 0)),
                  pl.BlockSpec((d, eb), lambda t, e: (0, e)),
                  rowspec, rowspec, pspec, pspec,
                  pl.BlockSpec((tt, d), lambda t, e: (t, 0)),
                  pl.BlockSpec((1, 1, d), lambda t, e: (t // tpb, 0, 5)),
                  pl.BlockSpec((1, d), lambda t, e: (0, 0))],
        out_specs=pl.BlockSpec((tt, d), lambda t, e: (t, 0)),
        out_shape=jax.ShapeDtypeStruct((n, d), F32),
        scratch_shapes=[pltpu.VMEM((d, tt), F32),
                        pltpu.VMEM((gq, tt), F32),
                        pltpu.VMEM((gq, tt), BF16)],
        compiler_params=_cparams(("arbitrary", "arbitrary")),
        name="peer",
    )(h2t, u_bf, vt_bf, cnt, e1, rank, e2, x1, mod3, fg)


def _retention_tables():
    hcount = RET_HEADS
    c = CHUNK
    gamma = 1.0 - 2.0 ** (-5.0 - jnp.arange(hcount, dtype=F32))
    log_g = jnp.log(gamma)
    idx = jnp.arange(c, dtype=F32)
    diff = idx[:, None] - idx[None, :]
    din = jnp.where((diff >= 0)[None], jnp.exp(log_g[:, None, None] * jnp.maximum(diff, 0.0)[None]), 0.0)
    dq = jnp.exp(log_g[:, None] * (idx[None, :] + 1.0))
    dk = jnp.exp(log_g[:, None] * (c - 1.0 - idx[None, :]))
    dc = jnp.exp(log_g * c)
    dq = jnp.broadcast_to(dq[:, :, None], (hcount, c, RET_V_DIM))
    dk = jnp.broadcast_to(dk[:, :, None], (hcount, c, RET_QK_DIM))
    dc = jnp.broadcast_to(dc[:, None, None], (hcount, RET_QK_DIM, RET_V_DIM))
    return din.astype(F32), dq.astype(F32), dk.astype(F32), dc.astype(F32)


def _rotary_tables(seq):
    half = RET_QK_DIM // 2
    pos = jnp.arange(seq, dtype=F32)
    inv = ROPE_BASE ** (-jnp.arange(half, dtype=F32) * 2.0 / RET_QK_DIM)
    ang = pos[:, None] * inv[None, :]
    cos = jnp.cos(ang)
    sin = jnp.sin(ang)
    return jnp.concatenate([cos, cos], axis=-1), jnp.concatenate([-sin, sin], axis=-1)


def kernel(x, c, w_ada, b_ada, norm1_g, w_in, ret_gn_g, sgu_ln_g, sgu_ln_b, sgu_w, sgu_b, w_ret_out,
           w_sgu_out, w_out, norm2_g, peer_w_q, peer_sub_keys, peer_u, peer_v, final_g):
    bsz, seq, d = x.shape
    n = bsz * seq
    depth = w_ada.shape[0]
    assert d == D_MODEL and seq % TM_PROJ == 0 and seq % PEER_TOKENS == 0 and seq % TOPK_TOKENS == 0
    cos_t, sin_t = _rotary_tables(seq)
    din, dq, dk, dc = _retention_tables()
    xc = x.reshape(n, d)
    for l in range(depth):
        mod3 = _ada(c, w_ada[l], b_ada[l]).reshape(bsz, 1, N_MOD * d)
        proj = _inproj(xc, mod3, norm1_g[l].reshape(1, d), cos_t, sin_t,
                       sgu_ln_g[l].reshape(1, d), sgu_ln_b[l].reshape(1, d), w_in[l].astype(BF16), seq)
        bs = jnp.broadcast_to(sgu_b[l][:, :, None], (SGU_GROUPS, CHUNK, SGU_GROUP_DIM))
        ya, ys = _chunk(proj, din, dq, dk, dc, ret_gn_g[l].reshape(1, d), sgu_w[l], bs, bsz, seq)
        x1, h2t = _merge(ya, ys, proj, xc, mod3, norm2_g[l].reshape(1, d), w_ret_out[l].astype(BF16),
                        w_sgu_out[l].astype(BF16), w_out[l].astype(BF16), seq)
        wq_t = peer_w_q[l].T.astype(BF16)
        keys = peer_sub_keys[l].reshape(PEER_HEADS * 2, PEER_N_KEYS, -1).astype(BF16)
        cnt, e1, rank, e2 = _topk(h2t, wq_t, keys)
        assert depth == 1
        xc = _peer(h2t, peer_u[l].astype(BF16), peer_v[l].T.astype(BF16), cnt, e1, rank, e2, x1, mod3,
                   final_g.reshape(1, d), seq)
    return xc.reshape(bsz, seq, d)
```

```python
import functools
import math

import jax
import jax.numpy as jnp
from jax import lax
from jax.experimental import pallas as pl
from jax.experimental.pallas import tpu as pltpu

F32 = jnp.float32
BF16 = jnp.bfloat16

D_MODEL = 1024
RET_HEADS = 4
RET_QK_DIM = 128
RET_V_DIM = 256
CHUNK = 128
SGU_GROUPS = 4
SGU_GROUP_DIM = 256
IN_WIDTH = 7168
PEER_HEADS = 8
PEER_N_KEYS = 128
PEER_TOPK = 16
N_MOD = 6
EPS = 1e-6
ROPE_BASE = 10000.0
NEG_INF = float("-inf")

VMEM_LIMIT_BYTES = 56 * 1024 * 1024

TM_PROJ = 512
CHUNK_BATCHES = 8
TOPK_TOKENS = 1024
PEER_TOKENS = 1024
PEER_EXPERTS = 1024
PEER_GROUP = 1024
PEER_PASS = 256
LANES = 128
SUBLANES = 8
PACKED_ROWS = 16


def _gelu(x):
    return 0.5 * x * (1.0 + lax.erf(x * (1.0 / math.sqrt(2.0))))


def _rms(x, g):
    ms = jnp.mean(x * x, axis=-1, keepdims=True)
    return x * lax.rsqrt(ms + EPS) * g


def _cparams(sem):
    return pltpu.CompilerParams(dimension_semantics=sem, vmem_limit_bytes=VMEM_LIMIT_BYTES)


def _ada_kernel(c_ref, w_ref, b_ref, o_ref):
    c = c_ref[...]
    ca = c * jax.nn.sigmoid(c)
    o_ref[...] = jnp.dot(ca.astype(BF16), w_ref[...].astype(BF16),
                         preferred_element_type=F32) + b_ref[...]


def _ada(c, w, b):
    bsz, d = c.shape
    n_out = w.shape[1]
    return pl.pallas_call(
        _ada_kernel,
        grid=(n_out // d,),
        in_specs=[pl.BlockSpec((bsz, d), lambda j: (0, 0)),
                  pl.BlockSpec((d, d), lambda j: (0, j)),
                  pl.BlockSpec((1, d), lambda j: (0, j))],
        out_specs=pl.BlockSpec((bsz, d), lambda j: (0, j)),
        out_shape=jax.ShapeDtypeStruct((bsz, n_out), F32),
        compiler_params=_cparams(("arbitrary",)),
        name="ada",
    )(c, w, b.reshape(1, n_out))


def _inproj_kernel(x_ref, sh_ref, sc_ref, g_ref, cos_ref, sin_ref, lng_ref, lnb_ref, w_ref, o_ref):
    h = (_rms(x_ref[...], g_ref[...]) * (1.0 + sc_ref[0]) + sh_ref[0]).astype(BF16)
    d = D_MODEL
    for j in range(IN_WIDTH // d):
        acc = jnp.dot(h, w_ref[:, j * d:(j + 1) * d], preferred_element_type=F32)
        if j == 0:
            cos = cos_ref[...]
            sin = sin_ref[...]
            for hh in range(2 * RET_HEADS):
                blk = acc[:, hh * RET_QK_DIM:(hh + 1) * RET_QK_DIM]
                r = blk * cos + pltpu.roll(blk, RET_QK_DIM // 2, 1) * sin
                if hh >= RET_HEADS:
                    r = r * (RET_QK_DIM ** -0.5)
                o_ref[:, hh * RET_QK_DIM:(hh + 1) * RET_QK_DIM] = r.astype(BF16)
            continue
        if j == 1:
            res = acc
        elif j == 2:
            res = acc * jax.nn.sigmoid(acc)
        elif j == 3:
            res = _gelu(acc)
        elif j == 4:
            a = _gelu(acc)
            mu = jnp.mean(a, axis=-1, keepdims=True)
            ac = a - mu
            var = jnp.mean(ac * ac, axis=-1, keepdims=True)
            res = ac * lax.rsqrt(var + EPS) * lng_ref[...] + lnb_ref[...]
        else:
            res = jax.nn.sigmoid(acc)
        o_ref[:, j * d:(j + 1) * d] = res.astype(BF16)


def _inproj(x2d, mod3, g, cos_t, sin_t, ln_g, ln_b, w_bf, seq):
    n, d = x2d.shape
    tm = TM_PROJ
    tpb = seq // tm
    return pl.pallas_call(
        _inproj_kernel,
        grid=(n // tm,),
        in_specs=[pl.BlockSpec((tm, d), lambda i: (i, 0)),
                  pl.BlockSpec((1, 1, d), lambda i: (i // tpb, 0, 0)),
                  pl.BlockSpec((1, 1, d), lambda i: (i // tpb, 0, 1)),
                  pl.BlockSpec((1, d), lambda i: (0, 0)),
                  pl.BlockSpec((tm, RET_QK_DIM), lambda i: (i % tpb, 0)),
                  pl.BlockSpec((tm, RET_QK_DIM), lambda i: (i % tpb, 0)),
                  pl.BlockSpec((1, d), lambda i: (0, 0)),
                  pl.BlockSpec((1, d), lambda i: (0, 0)),
                  pl.BlockSpec((d, IN_WIDTH), lambda i: (0, 0), pipeline_mode=pl.Buffered(1))],
        out_specs=pl.BlockSpec((tm, IN_WIDTH), lambda i: (i, 0)),
        out_shape=jax.ShapeDtypeStruct((n, IN_WIDTH), BF16),
        compiler_params=_cparams(("arbitrary",)),
        name="inproj",
    )(x2d, mod3, mod3, g, cos_t, sin_t, ln_g, ln_b, w_bf)


def _chunk_kernel(qk_ref, v_ref, gs_ref, u_ref, vn_ref, din_ref, dq_ref, dk_ref, dc_ref,
                  gn_ref, ws_ref, bs_ref, ya_ref, ys_ref, state_ref):
    @pl.when(pl.program_id(1) == 0)
    def _():
        state_ref[...] = jnp.zeros_like(state_ref)

    row = lax.broadcasted_iota(jnp.int32, (CHUNK, CHUNK), 0)
    col = lax.broadcasted_iota(jnp.int32, (CHUNK, CHUNK), 1)
    ws = [jnp.where(row >= col, ws_ref[gg], 0.0).astype(BF16) for gg in range(SGU_GROUPS)]

    for b in range(qk_ref.shape[0]):
        for hh in range(RET_HEADS):
            q = qk_ref[b, :, hh * RET_QK_DIM:(hh + 1) * RET_QK_DIM]
            k = qk_ref[b, :, (RET_HEADS + hh) * RET_QK_DIM:(RET_HEADS + hh + 1) * RET_QK_DIM]
            v = v_ref[b, :, hh * RET_V_DIM:(hh + 1) * RET_V_DIM]
            state = state_ref[b, hh]
            s = lax.dot_general(q, k, (((1,), (1,)), ((), ())), preferred_element_type=F32) * din_ref[hh]
            inner = jnp.dot(s.astype(BF16), v, preferred_element_type=F32)
            cross = jnp.dot(q, state.astype(BF16), preferred_element_type=F32) * dq_ref[hh]
            kd_t = (k.astype(F32) * dk_ref[hh]).T.astype(BF16)
            state_ref[b, hh] = state * dc_ref[hh] + jnp.dot(kd_t, v, preferred_element_type=F32)
            y = inner + cross
            mu = jnp.mean(y, axis=-1, keepdims=True)
            yc = y - mu
            var = jnp.mean(yc * yc, axis=-1, keepdims=True)
            sl = slice(hh * RET_V_DIM, (hh + 1) * RET_V_DIM)
            yn = yc * lax.rsqrt(var + EPS) * gn_ref[:, sl]
            ya_ref[b, :, sl] = (gs_ref[b, :, sl].astype(F32) * yn).astype(BF16)

        for gg in range(SGU_GROUPS):
            sl = slice(gg * SGU_GROUP_DIM, (gg + 1) * SGU_GROUP_DIM)
            mixed = jnp.dot(ws[gg], vn_ref[b, :, sl], preferred_element_type=F32) + bs_ref[gg]
            ys_ref[b, :, sl] = (u_ref[b, :, sl].astype(F32) * mixed).astype(BF16)


def _chunk(proj, din, dq, dk, dc, gn, ws, bs, bsz, seq):
    n = proj.shape[0]
    nc = seq // CHUNK
    d = D_MODEL
    nb = math.gcd(bsz, CHUNK_BATCHES)
    proj3 = proj.reshape(bsz, seq, IN_WIDTH)

    def col(j):
        return pl.BlockSpec((nb, CHUNK, d), lambda b, c: (b, c, j))

    def const(shape):
        return pl.BlockSpec(shape, lambda b, c: (0,) * len(shape))

    ya, ys = pl.pallas_call(
        _chunk_kernel,
        grid=(bsz // nb, nc),
        in_specs=[col(0), col(1), col(2), col(3), col(4),
                  const(din.shape), const(dq.shape), const(dk.shape), const(dc.shape),
                  const(gn.shape), const(ws.shape), const(bs.shape)],
        out_specs=[col(0), col(0)],
        out_shape=[jax.ShapeDtypeStruct((bsz, seq, d), BF16), jax.ShapeDtypeStruct((bsz, seq, d), BF16)],
        scratch_shapes=[pltpu.VMEM((nb, RET_HEADS, RET_QK_DIM, RET_V_DIM), F32)],
        compiler_params=_cparams(("arbitrary", "arbitrary")),
        name="chunk",
    )(proj3, proj3, proj3, proj3, proj3, din, dq, dk, dc, gn, ws, bs)
    return ya.reshape(n, d), ys.reshape(n, d)


def _merge_kernel(ya_ref, ys_ref, sa_ref, sb_ref, x_ref, g1_ref, sh2_ref, sc2_ref, n2_ref,
                  wa_ref, wb_ref, wo_ref, x1_ref, h2t_ref):
    ba = jnp.dot(ya_ref[...], wa_ref[...], preferred_element_type=F32)
    bb = jnp.dot(ys_ref[...], wb_ref[...], preferred_element_type=F32)
    merged = sa_ref[...].astype(F32) * ba + sb_ref[...].astype(F32) * bb
    mo = jnp.dot(merged.astype(BF16), wo_ref[...], preferred_element_type=F32)
    x1 = x_ref[...] + g1_ref[0] * mo
    x1_ref[...] = x1
    h2t_ref[...] = (_rms(x1, n2_ref[...]) * (1.0 + sc2_ref[0]) + sh2_ref[0]).T.astype(BF16)


def _merge(ya, ys, proj, x2d, mod3, n2, wa, wb, wo, seq):
    n, d = x2d.shape
    tm = TM_PROJ
    tpb = seq // tm

    def modspec(k):
        return pl.BlockSpec((1, 1, d), lambda i: (i // tpb, 0, k))

    def wspec():
        return pl.BlockSpec((d, d), lambda i: (0, 0))

    return pl.pallas_call(
        _merge_kernel,
        grid=(n // tm,),
        in_specs=[pl.BlockSpec((tm, d), lambda i: (i, 0)),
                  pl.BlockSpec((tm, d), lambda i: (i, 0)),
                  pl.BlockSpec((tm, d), lambda i: (i, 5)),
                  pl.BlockSpec((tm, d), lambda i: (i, 6)),
                  pl.BlockSpec((tm, d), lambda i: (i, 0)),
                  modspec(2), modspec(3), modspec(4),
                  pl.BlockSpec((1, d), lambda i: (0, 0)),
                  wspec(), wspec(), wspec()],
        out_specs=[pl.BlockSpec((tm, d), lambda i: (i, 0)),
                   pl.BlockSpec((d, tm), lambda i: (0, i))],
        out_shape=[jax.ShapeDtypeStruct((n, d), F32), jax.ShapeDtypeStruct((d, n), BF16)],
        compiler_params=_cparams(("arbitrary",)),
        name="merge",
    )(ya, ys, proj, proj, x2d, mod3, mod3, mod3, n2, wa, wb, wo)


def _sort_network(n):
    pairs = []
    p = 1
    while p < n:
        k = p
        while k >= 1:
            for j in range(k % p, n - k, 2 * k):
                for i in range(min(k, n - j - k)):
                    if (i + j) // (2 * p) == (i + j + k) // (2 * p):
                        pairs.append((i + j, i + j + k))
            k //= 2
        p *= 2
    return pairs


def _sort_desc(vs):
    vs = list(vs)
    for i, j in _sort_network(PEER_TOPK):
        if j < len(vs):
            vs[i], vs[j] = jnp.maximum(vs[i], vs[j]), jnp.minimum(vs[i], vs[j])
    return vs


def _merge_top16_over_sublanes(vs):
    n = PEER_TOPK
    vs = list(vs) + [None] * (n - len(vs))
    for shift in (4, 2, 1):
        w = []
        for k in range(n):
            a, b = vs[k], vs[n - 1 - k]
            b = None if b is None else pltpu.roll(b, shift, 0)
            w.append(a if b is None else (b if a is None else jnp.maximum(a, b)))
        d = n // 2
        while d >= 1:
            for k in range(n):
                if k & d == 0:
                    w[k], w[k + d] = jnp.maximum(w[k], w[k + d]), jnp.minimum(w[k], w[k + d])
            d //= 2
        vs = w
    return vs


def _prefix_count(a, pred):
    assert len(a) == PEER_TOPK == 16
    m1 = pred(a[7])
    m2 = pred(jnp.where(m1, a[11], a[3]))
    m3 = pred(jnp.where(m1, jnp.where(m2, a[13], a[9]), jnp.where(m2, a[5], a[1])))
    hi = jnp.where(m2, jnp.where(m3, a[14], a[12]), jnp.where(m3, a[10], a[8]))
    lo = jnp.where(m2, jnp.where(m3, a[6], a[4]), jnp.where(m3, a[2], a[0]))
    m4 = pred(jnp.where(m1, hi, lo))
    c = (jnp.where(m1, 8.0, 0.0) + jnp.where(m2, 4.0, 0.0)) + (jnp.where(m3, 2.0, 0.0) + jnp.where(m4, 1.0, 0.0))
    return jnp.where(pred(a[15]), 16.0, c)


def _topk_kernel(h2t_ref, wq_ref, keys_ref, cnt_ref, e1_ref, rank_ref, e2_ref, qt_ref):
    nk = PEER_N_KEYS
    ntb = TOPK_TOKENS // LANES
    nv = nk // SUBLANES
    qt_ref[...] = jnp.dot(wq_ref[...], h2t_ref[...], preferred_element_type=F32).astype(BF16)

    def pack(rep, sub_iota):
        out = rep[SUBLANES - 1]
        for r in range(SUBLANES - 2, -1, -1):
            out = jnp.where(sub_iota == r, rep[r], out)
        return out

    def head_body(hh, carry):
        sub_iota = lax.broadcasted_iota(jnp.int32, (SUBLANES, LANES), 0)
        sts = []
        for p in range(2):
            off = pl.multiple_of((hh * 2 + p) * nk, nk)
            sts.append(jnp.dot(keys_ref[hh * 2 + p], qt_ref[pl.ds(off, nk), :],
                               preferred_element_type=F32))
        for tb in range(ntb):
            lsl = slice(tb * LANES, (tb + 1) * LANES)
            rows = [[sts[p][r * SUBLANES:(r + 1) * SUBLANES, lsl] for r in range(nv)] for p in range(2)]
            a1 = _merge_top16_over_sublanes(_sort_desc(rows[0]))
            a2 = _merge_top16_over_sublanes(_sort_desc(rows[1]))
            a1lo, a1hi = pack(a1[:SUBLANES], sub_iota), pack(a1[SUBLANES:], sub_iota)
            a2lo, a2hi = pack(a2[:SUBLANES], sub_iota), pack(a2[SUBLANES:], sub_iota)
            tail = sub_iota >= 2
            cands = [a1lo + a2[0], a1hi + a2[0], a1lo + a2[1],
                     jnp.where(tail, a1[0] + a2lo, NEG_INF), a1[0] + a2hi,
                     jnp.where(tail, a1[1] + a2lo, NEG_INF)]
            cands += [jnp.where(tail, a1lo + a2[l], NEG_INF) for l in (2, 3, 4)]
            top = _merge_top16_over_sublanes(_sort_desc(cands))
            tau = top[PEER_TOPK - 1]
            z = functools.reduce(lambda x, y: x + y, [jnp.exp(t - top[0]) for t in top])
            rz = 1.0 / z
            for rp in range(nv // 2):
                ranks, e2s = [], []
                for r in (2 * rp, 2 * rp + 1):
                    rsl = slice(r * SUBLANES, (r + 1) * SUBLANES)
                    s1v, s2v = rows[0][r], rows[1][r]
                    cnt = _prefix_count(a2, lambda t: s1v + t >= tau)
                    rank = _prefix_count(a2, lambda t: t > s2v)
                    cnt_ref[tb, hh, rsl, :] = cnt
                    e1_ref[tb, hh, rsl, :] = jnp.exp(s1v - a1[0]) * rz
                    ranks.append(rank)
                    e2s.append(jnp.exp(s2v - a2[0]))
                psl = slice(rp * SUBLANES, (rp + 1) * SUBLANES)
                rank_ref[tb, hh, psl, :] = pltpu.bitcast(jnp.concatenate(ranks, axis=0).astype(BF16), jnp.uint32)
                e2_ref[tb, hh, psl, :] = pltpu.bitcast(jnp.concatenate(e2s, axis=0).astype(BF16), jnp.uint32)
        return carry

    lax.fori_loop(0, PEER_HEADS, head_body, 0)


def _topk(h2t, wq_t, keys):
    d, n = h2t.shape
    tt = TOPK_TOKENS
    ntb = tt // LANES
    nb = n // LANES
    shape = (nb, PEER_HEADS, PEER_N_KEYS, LANES)
    pshape = (nb, PEER_HEADS, PEER_N_KEYS // 2, LANES)
    bigspec = pl.BlockSpec((ntb,) + shape[1:], lambda i: (i, 0, 0, 0))
    pspec = pl.BlockSpec((ntb,) + pshape[1:], lambda i: (i, 0, 0, 0))
    return pl.pallas_call(
        _topk_kernel,
        grid=(n // tt,),
        in_specs=[pl.BlockSpec((d, tt), lambda i: (0, i)),
                  pl.BlockSpec(wq_t.shape, lambda i: (0, 0)),
                  pl.BlockSpec(keys.shape, lambda i: (0, 0, 0))],
        out_specs=[bigspec, bigspec, pspec, pspec],
        out_shape=[jax.ShapeDtypeStruct(shape, F32), jax.ShapeDtypeStruct(shape, F32),
                   jax.ShapeDtypeStruct(pshape, jnp.uint32), jax.ShapeDtypeStruct(pshape, jnp.uint32)],
        scratch_shapes=[pltpu.VMEM((wq_t.shape[0], tt), BF16)],
        compiler_params=_cparams(("arbitrary",)),
        name="topk",
    )(h2t, wq_t, keys)


def _peer_kernel(h2t_ref, u_ref, vt_ref, cnt_ref, e1_ref, rank_ref, e2_ref, x1_ref, g2_ref, fg_ref,
                 o_ref, acc_ref, st_ref, a_ref):
    e = pl.program_id(1)
    ntb = PEER_TOKENS // LANES
    nk = PEER_N_KEYS
    npass = PEER_GROUP // PEER_PASS
    ni = PEER_PASS // nk
    nrow = nk // PACKED_ROWS
    nq = PEER_EXPERTS // PEER_GROUP

    @pl.when(e == 0)
    def _():
        acc_ref[...] = jnp.zeros_like(acc_ref)

    def packed_row(ref, tb, hh, key):
        return jnp.broadcast_to(ref[tb, hh, key:key + 1, :], (PACKED_ROWS, LANES)).astype(BF16)

    def gate_pass(g, ps):
        key0 = g * (PEER_GROUP // nk) + ps * ni
        rbase = ps * PEER_PASS
        for tb in range(ntb):
            lsl = slice(tb * LANES, (tb + 1) * LANES)
            gs = [[jnp.zeros((PACKED_ROWS, LANES), BF16) for _ in range(nrow)] for _ in range(ni)]
            for hh in range(PEER_HEADS):
                cnts = [packed_row(cnt_ref, tb, hh, key0 + ii) for ii in range(ni)]
                e1s = [packed_row(e1_ref, tb, hh, key0 + ii) for ii in range(ni)]
                for r in range(nrow):
                    wsl = slice(r * SUBLANES, (r + 1) * SUBLANES)
                    rk = pltpu.bitcast(rank_ref[tb, hh, wsl, :], BF16)
                    e2v = pltpu.bitcast(e2_ref[tb, hh, wsl, :], BF16)
                    for ii in range(ni):
                        gs[ii][r] = gs[ii][r] + jnp.where(rk < cnts[ii], e2v, jnp.zeros_like(e2v)) * e1s[ii]
            for ii in range(ni):
                for r in range(nrow):
                    r0 = rbase + ii * nk + r * PACKED_ROWS
                    rows = slice(r0, r0 + PACKED_ROWS)
                    words = slice(r0 // 2, r0 // 2 + SUBLANES)
                    a_ref[rows, lsl] = _gelu(pltpu.bitcast(st_ref[words, lsl], BF16)) * gs[ii][r]

    def gates(g):
        for ps in range(npass):
            pl.when(e >= 0)(functools.partial(gate_pass, g, ps))

    for g in range(nq):
        gsl = slice(g * PEER_GROUP, (g + 1) * PEER_GROUP)
        st_ref[...] = pltpu.bitcast(
            jnp.dot(u_ref[gsl, :], h2t_ref[...], preferred_element_type=F32).astype(BF16), jnp.uint32)
        gates(g)
        acc_ref[...] += jnp.dot(vt_ref[:, gsl], a_ref[...], preferred_element_type=F32)

    @pl.when(e == pl.num_programs(1) - 1)
    def _():
        x2 = x1_ref[...] + g2_ref[0] * acc_ref[...].T
        o_ref[...] = _rms(x2, fg_ref[...])


def _peer(h2t, u_bf, vt_bf, cnt, e1, rank, e2, x1, mod3, fg, seq):
    d, n = h2t.shape
    tt = PEER_TOKENS
    eb = PEER_EXPERTS
    gq = PEER_GROUP
    ntb = tt // LANES
    tpb = seq // tt
    n_exp = u_bf.shape[0]
    rowspec = pl.BlockSpec((ntb, PEER_HEADS, eb // PEER_N_KEYS, LANES), lambda t, e: (t, 0, e, 0))
    pspec = pl.BlockSpec((ntb, PEER_HEADS, PEER_N_KEYS // 2, LANES), lambda t, e: (t, 0, 0, 0))
    assert eb // PEER_N_KEYS == SUBLANES
    return pl.pallas_call(
        _peer_kernel,
        grid=(n // tt, n_exp // eb),
        in_specs=[pl.BlockSpec((d, tt), lambda t, e: (0, t)),
                  pl.BlockSpec((eb, d), lambda t, e: (e, 0)),
                  pl.BlockSpec((d, eb), lambda t, e: (0, e)),
                  rowspec, rowspec, pspec, pspec,
                  pl.BlockSpec((tt, d), lambda t, e: (t, 0)),
                  pl.BlockSpec((1, 1, d), lambda t, e: (t // tpb, 0, 5)),
                  pl.BlockSpec((1, d), lambda t, e: (0, 0))],
        out_specs=pl.BlockSpec((tt, d), lambda t, e: (t, 0)),
        out_shape=jax.ShapeDtypeStruct((n, d), F32),
        scratch_shapes=[pltpu.VMEM((d, tt), F32),
                        pltpu.VMEM((gq // 2, tt), jnp.uint32),
                        pltpu.VMEM((gq, tt), BF16)],
        compiler_params=_cparams(("arbitrary", "arbitrary")),
        name="peer",
    )(h2t, u_bf, vt_bf, cnt, e1, rank, e2, x1, mod3, fg)


def _retention_tables():
    hcount = RET_HEADS
    c = CHUNK
    gamma = 1.0 - 2.0 ** (-5.0 - jnp.arange(hcount, dtype=F32))
    log_g = jnp.log(gamma)
    idx = jnp.arange(c, dtype=F32)
    diff = idx[:, None] - idx[None, :]
    din = jnp.where((diff >= 0)[None], jnp.exp(log_g[:, None, None] * jnp.maximum(diff, 0.0)[None]), 0.0)
    dq = jnp.exp(log_g[:, None] * (idx[None, :] + 1.0))
    dk = jnp.exp(log_g[:, None] * (c - 1.0 - idx[None, :]))
    dc = jnp.exp(log_g * c)
    dq = jnp.broadcast_to(dq[:, :, None], (hcount, c, RET_V_DIM))
    dk = jnp.broadcast_to(dk[:, :, None], (hcount, c, RET_QK_DIM))
    dc = jnp.broadcast_to(dc[:, None, None], (hcount, RET_QK_DIM, RET_V_DIM))
    return din.astype(F32), dq.astype(F32), dk.astype(F32), dc.astype(F32)


def _rotary_tables(seq):
    half = RET_QK_DIM // 2
    pos = jnp.arange(seq, dtype=F32)
    inv = ROPE_BASE ** (-jnp.arange(half, dtype=F32) * 2.0 / RET_QK_DIM)
    ang = pos[:, None] * inv[None, :]
    cos = jnp.cos(ang)
    sin = jnp.sin(ang)
    return jnp.concatenate([cos, cos], axis=-1), jnp.concatenate([-sin, sin], axis=-1)


def kernel(x, c, w_ada, b_ada, norm1_g, w_in, ret_gn_g, sgu_ln_g, sgu_ln_b, sgu_w, sgu_b, w_ret_out,
           w_sgu_out, w_out, norm2_g, peer_w_q, peer_sub_keys, peer_u, peer_v, final_g):
    bsz, seq, d = x.shape
    n = bsz * seq
    depth = w_ada.shape[0]
    assert d == D_MODEL and seq % TM_PROJ == 0 and seq % PEER_TOKENS == 0 and seq % TOPK_TOKENS == 0
    cos_t, sin_t = _rotary_tables(seq)
    din, dq, dk, dc = _retention_tables()
    xc = x.reshape(n, d)
    for l in range(depth):
        mod3 = _ada(c, w_ada[l], b_ada[l]).reshape(bsz, 1, N_MOD * d)
        proj = _inproj(xc, mod3, norm1_g[l].reshape(1, d), cos_t, sin_t,
                       sgu_ln_g[l].reshape(1, d), sgu_ln_b[l].reshape(1, d), w_in[l].astype(BF16), seq)
        bs = jnp.broadcast_to(sgu_b[l][:, :, None], (SGU_GROUPS, CHUNK, SGU_GROUP_DIM))
        ya, ys = _chunk(proj, din, dq, dk, dc, ret_gn_g[l].reshape(1, d), sgu_w[l], bs, bsz, seq)
        x1, h2t = _merge(ya, ys, proj, xc, mod3, norm2_g[l].reshape(1, d), w_ret_out[l].astype(BF16),
                        w_sgu_out[l].astype(BF16), w_out[l].astype(BF16), seq)
        wq_t = peer_w_q[l].T.astype(BF16)
        keys = peer_sub_keys[l].reshape(PEER_HEADS * 2, PEER_N_KEYS, -1).astype(BF16)
        cnt, e1, rank, e2 = _topk(h2t, wq_t, keys)
        assert depth == 1
        xc = _peer(h2t, peer_u[l].astype(BF16), peer_v[l].T.astype(BF16), cnt, e1, rank, e2, x1, mod3,
                   final_g.reshape(1, d), seq)
    return xc.reshape(bsz, seq, d)
```
